```python
import jax, jax.numpy as jnp
from jax import lax
import numpy as np

D_MODEL = 1024
BATCH = 4
SEQ = 8192
DEPTH = 1

N_ATTN_HEADS = 8
HEAD_DIM = 64
ATTN_WIDTH = N_ATTN_HEADS * HEAD_DIM
CONV_GROUPS = 8
CONV_WIDTH = 512
CONV_K = 3
D_FF = 4 * D_MODEL
PLE_DIM = 256
Q_BLOCK = 128
EPS = 1e-6
SPLIT_SIZES = (ATTN_WIDTH, ATTN_WIDTH, ATTN_WIDTH, CONV_WIDTH, CONV_WIDTH, CONV_WIDTH, D_MODEL, D_MODEL)
SPLIT_POINTS = tuple(int(v) for v in np.cumsum(SPLIT_SIZES)[:-1])
D_IN = sum(SPLIT_SIZES)

kernel_name = 'hybrid_stickbreak_shortconv_block'


def rms_norm(x, g):
    xf = x.astype(jnp.float32)
    var = jnp.mean(xf * xf, axis=-1, keepdims=True)
    return (xf * lax.rsqrt(var + EPS) * g.astype(jnp.float32)).astype(x.dtype)


def stick_breaking_attention(q, k, v):
    b, h, s, dh = q.shape
    nblk = s // Q_BLOCK
    scale = dh ** -0.5
    kf = k.astype(jnp.float32)
    vf = v.astype(jnp.float32)
    q_blocks = q.reshape(b, h, nblk, Q_BLOCK, dh).transpose(2, 0, 1, 3, 4)
    key_pos = jnp.arange(s, dtype=jnp.int32)
    starts = jnp.arange(nblk, dtype=jnp.int32) * Q_BLOCK

    def block(args):
        qb, start = args
        z = jnp.einsum('bhqd,bhkd->bhqk', qb.astype(jnp.float32), kf) * scale
        q_pos = start + jnp.arange(Q_BLOCK, dtype=jnp.int32)
        causal = key_pos[None, :] < q_pos[:, None]
        log_beta = jax.nn.log_sigmoid(z)
        log_keep = jnp.where(causal, log_beta - z, 0.0)
        between = lax.cumsum(log_keep, axis=3, reverse=True) - log_keep
        w = jnp.where(causal, jnp.exp(log_beta + between), 0.0)
        return jnp.einsum('bhqk,bhkd->bhqd', w, vf)

    out = lax.map(block, (q_blocks, starts))
    return out.transpose(1, 2, 0, 3, 4).reshape(b, h, s, dh).astype(q.dtype)


def causal_depthwise_conv(u, w):
    c = u.shape[-1]
    return lax.conv_general_dilated(
        u, w[:, None, :].astype(u.dtype), window_strides=(1,),
        padding=((CONV_K - 1, 0),), dimension_numbers=('NWC', 'WIO', 'NWC'),
        feature_group_count=c)


def setup_inputs(seed: int = 0) -> dict:
    key = jax.random.key(seed)
    ks = jax.random.split(key, 20)
    f32 = jnp.float32

    def nrm(k, shape, fan_in):
        return jax.random.normal(k, shape, f32) * (fan_in ** -0.5)

    def gain(k, shape):
        return jnp.ones(shape, f32) + 0.05 * jax.random.normal(k, shape, f32)

    return {
        'x': jax.random.normal(ks[0], (BATCH, SEQ, D_MODEL), f32),
        'p': jax.random.normal(ks[1], (DEPTH, BATCH, SEQ, PLE_DIM), f32),
        'g_pre_mix': gain(ks[2], (DEPTH, D_MODEL)),
        'w_in': nrm(ks[3], (DEPTH, D_MODEL, D_IN), D_MODEL),
        'b_gate': 0.1 * jax.random.normal(ks[4], (DEPTH, 2 * D_MODEL), f32),
        'w_conv': nrm(ks[5], (DEPTH, CONV_K, CONV_WIDTH), CONV_K),
        'w_attn_out': nrm(ks[6], (DEPTH, ATTN_WIDTH, D_MODEL), ATTN_WIDTH),
        'w_conv_out': nrm(ks[7], (DEPTH, CONV_WIDTH, D_MODEL), CONV_WIDTH),
        'w_o': nrm(ks[8], (DEPTH, D_MODEL, D_MODEL), D_MODEL),
        'g_post_mix': gain(ks[9], (DEPTH, D_MODEL)),
        'g_pre_mlp': gain(ks[10], (DEPTH, D_MODEL)),
        'w_up': nrm(ks[11], (DEPTH, D_MODEL, D_FF), D_MODEL),
        'w_down': nrm(ks[12], (DEPTH, D_FF, D_MODEL), D_FF),
        'g_post_mlp': gain(ks[13], (DEPTH, D_MODEL)),
        'g_ple': gain(ks[14], (DEPTH, D_MODEL)),
        'w_ple_gate': nrm(ks[15], (DEPTH, D_MODEL, D_MODEL), D_MODEL),
        'w_ple_proj': nrm(ks[16], (DEPTH, PLE_DIM, D_MODEL), PLE_DIM),
    }


def reference(x, p, g_pre_mix, w_in, b_gate, w_conv, w_attn_out, w_conv_out, w_o,
              g_post_mix, g_pre_mlp, w_up, w_down, g_post_mlp, g_ple, w_ple_gate, w_ple_proj):
    bsz, seq, _ = x.shape
    for i in range(DEPTH):
        h = rms_norm(x, g_pre_mix[i])
        proj = h @ w_in[i]
        q, k, v, cb, cc, cu, ga, gc = jnp.split(proj, SPLIT_POINTS, axis=-1)

        def heads(t):
            return t.reshape(bsz, seq, N_ATTN_HEADS, HEAD_DIM).transpose(0, 2, 1, 3)

        o = stick_breaking_attention(heads(q), heads(k), heads(v))
        o = o.transpose(0, 2, 1, 3).reshape(bsz, seq, ATTN_WIDTH)
        y_attn = o @ w_attn_out[i]

        y_conv = (cb * causal_depthwise_conv(cc * cu, w_conv[i])) @ w_conv_out[i]

        gates = jax.nn.sigmoid(jnp.concatenate([ga, gc], axis=-1) + b_gate[i])
        gate_attn, gate_conv = jnp.split(gates, 2, axis=-1)
        mixed = (gate_attn * y_attn + gate_conv * y_conv) @ w_o[i]
        x = x + rms_norm(mixed, g_post_mix[i])

        h = rms_norm(x, g_pre_mlp[i])
        f = jnp.square(jax.nn.relu(h @ w_up[i])) @ w_down[i]
        x = x + rms_norm(f, g_post_mlp[i])

        ple_gate = jax.nn.sigmoid(rms_norm(x, g_ple[i]) @ w_ple_gate[i])
        x = x + ple_gate * (p[i] @ w_ple_proj[i])
    return x
```

```python
import functools
import math

import jax
import jax.numpy as jnp
from jax import lax
from jax.experimental import pallas as pl
from jax.experimental.pallas import tpu as pltpu

D_MODEL = 1024
N_HEADS = 8
HEAD_DIM = 64
ATTN_WIDTH = N_HEADS * HEAD_DIM
CONV_WIDTH = 512
CONV_K = 3
D_FF = 4 * D_MODEL
PLE_DIM = 256
EPS = 1e-6

LANES = 128
SUBLANES = 8
VMEM_LIMIT_BYTES = 56 * 1024 * 1024

TOKEN_TILE = 512
KEY_BLOCK = 256
QUERY_BLOCK = 512
FF_CHUNK = 1024
HEAD_PAD = 2 * HEAD_DIM

BF16 = jnp.bfloat16
F32 = jnp.float32


def _rms_norm(xf, g):
    var = jnp.mean(xf * xf, axis=-1, keepdims=True)
    return xf * lax.rsqrt(var + EPS) * g


def _sigmoid(v):
    return 1.0 / (1.0 + jnp.exp(-v))


def _resident(shape):
    zeros = (0,) * len(shape)
    return pl.BlockSpec(shape, lambda *_: zeros, pipeline_mode=pl.Buffered(1))


def _pre_kernel(x_ref, g_ref, wt_ref, wn_ref, bg_ref, wc_ref, wco_ref,
                qt_ref, k_ref, vt_ref, gate_ref, gconv_ref, ubuf_ref, *, q_scale):
    tm = x_ref.shape[1]
    h = _rms_norm(x_ref[0], g_ref[...]).astype(BF16)

    tr = lax.dot_general(wt_ref[...], h, (((1,), (1,)), ((), ())),
                         preferred_element_type=F32)
    zeros_half = jnp.zeros((HEAD_DIM, tm), BF16)
    for hd in range(N_HEADS):
        q_h = (tr[hd * HEAD_DIM:(hd + 1) * HEAD_DIM] * q_scale).astype(BF16)
        lo = (hd % 2) * HEAD_DIM
        qt_ref[0, hd, lo:lo + HEAD_DIM, :] = q_h
        qt_ref[0, hd, HEAD_DIM - lo:2 * HEAD_DIM - lo, :] = zeros_half
        v_h = tr[ATTN_WIDTH + hd * HEAD_DIM:ATTN_WIDTH + (hd + 1) * HEAD_DIM].astype(BF16)
        for c in range(tm // KEY_BLOCK):
            vt_ref[0, hd, c] = v_h[:, c * KEY_BLOCK:(c + 1) * KEY_BLOCK]

    k_ref[0] = jnp.dot(h, wn_ref[:, 0:ATTN_WIDTH],
                       preferred_element_type=F32).astype(BF16)

    c0 = ATTN_WIDTH
    conv3 = jnp.dot(h, wn_ref[:, c0:c0 + 3 * CONV_WIDTH], preferred_element_type=F32)
    cb = conv3[:, 0:CONV_WIDTH]
    u = conv3[:, CONV_WIDTH:2 * CONV_WIDTH] * conv3[:, 2 * CONV_WIDTH:3 * CONV_WIDTH]

    @pl.when(pl.program_id(1) == 0)
    def _():
        ubuf_ref[0:SUBLANES, :] = jnp.zeros((SUBLANES, CONV_WIDTH), F32)

    ubuf_ref[SUBLANES:SUBLANES + tm, :] = u
    u1 = ubuf_ref[SUBLANES - 1:SUBLANES - 1 + tm, :]
    u2 = ubuf_ref[SUBLANES - 2:SUBLANES - 2 + tm, :]
    ubuf_ref[0:SUBLANES, :] = u[tm - SUBLANES:tm, :]
    wc = wc_ref[...]
    conv = wc[0:1, :] * u2 + wc[1:2, :] * u1 + wc[2:3, :] * u
    y_conv = jnp.dot((cb * conv).astype(BF16), wco_ref[...],
                     preferred_element_type=F32)

    g0 = ATTN_WIDTH + 3 * CONV_WIDTH
    ga = jnp.dot(h, wn_ref[:, g0:g0 + D_MODEL], preferred_element_type=F32)
    gate_ref[0] = _sigmoid(ga + bg_ref[:, 0:D_MODEL])
    gc = jnp.dot(h, wn_ref[:, g0 + D_MODEL:g0 + 2 * D_MODEL], preferred_element_type=F32)
    gconv_ref[0] = _sigmoid(gc + bg_ref[:, D_MODEL:2 * D_MODEL]) * y_conv


def _pre_call(x, g, w_t, w_n, b_gate, w_conv, w_conv_out):
    bsz, seq, _ = x.shape
    tm = TOKEN_TILE
    q_scale = (HEAD_DIM ** -0.5) * math.log2(math.e)
    n_nat = w_n.shape[1]
    out_shape = (
        jax.ShapeDtypeStruct((bsz, N_HEADS, HEAD_PAD, seq), BF16),
        jax.ShapeDtypeStruct((bsz, seq, ATTN_WIDTH), BF16),
        jax.ShapeDtypeStruct((bsz, N_HEADS, seq // KEY_BLOCK, HEAD_DIM, KEY_BLOCK), BF16),
        jax.ShapeDtypeStruct((bsz, seq, D_MODEL), F32),
        jax.ShapeDtypeStruct((bsz, seq, D_MODEL), F32),
    )
    return pl.pallas_call(
        functools.partial(_pre_kernel, q_scale=q_scale),
        out_shape=out_shape,
        grid=(bsz, seq // tm),
        in_specs=[
            pl.BlockSpec((1, tm, D_MODEL), lambda b, i: (b, i, 0)),
            _resident((1, D_MODEL)),
            _resident((2 * ATTN_WIDTH, D_MODEL)),
            _resident((D_MODEL, n_nat)),
            _resident((1, 2 * D_MODEL)),
            _resident((CONV_K, CONV_WIDTH)),
            _resident((CONV_WIDTH, D_MODEL)),
        ],
        out_specs=(
            pl.BlockSpec((1, N_HEADS, HEAD_PAD, tm), lambda b, i: (b, 0, 0, i)),
            pl.BlockSpec((1, tm, ATTN_WIDTH), lambda b, i: (b, i, 0)),
            pl.BlockSpec((1, N_HEADS, tm // KEY_BLOCK, HEAD_DIM, KEY_BLOCK),
                         lambda b, i: (b, 0, i, 0, 0)),
            pl.BlockSpec((1, tm, D_MODEL), lambda b, i: (b, i, 0)),
            pl.BlockSpec((1, tm, D_MODEL), lambda b, i: (b, i, 0)),
        ),
        scratch_shapes=[pltpu.VMEM((tm + SUBLANES, CONV_WIDTH), F32)],
        compiler_params=pltpu.CompilerParams(
            dimension_semantics=("arbitrary", "arbitrary"),
            vmem_limit_bytes=VMEM_LIMIT_BYTES),
        name="pre_proj",
    )(x, g, w_t, w_n, b_gate, w_conv, w_conv_out)


def _softplus2(z):
    return jnp.maximum(z, 0.0) + jnp.log2(1.0 + jnp.exp2(-jnp.abs(z)))


def _attn_kernel(qt_ref, k_ref, vt_ref, o_ref):
    bq = qt_ref.shape[3]
    bk = KEY_BLOCK
    qi = pl.program_id(2)
    qt = qt_ref[0, 0]
    n_diag = bq // bk

    row = lax.broadcasted_iota(jnp.int32, (bk, bk), 0)
    col = lax.broadcasted_iota(jnp.int32, (bk, bk), 1)
    tri = (col >= row).astype(BF16)

    def block(j, carry, acc, masked_from):
        start = pl.multiple_of(j * bk, bk)
        kb = k_ref[0, pl.ds(start, bk), :]
        z = jnp.dot(kb, qt, preferred_element_type=F32)
        sp = _softplus2(z)
        if masked_from is not None:
            key_pos = lax.broadcasted_iota(jnp.int32, (bk, bq), 0) + masked_from * bk
            q_pos = lax.broadcasted_iota(jnp.int32, (bk, bq), 1)
            causal = key_pos < q_pos
            sp = jnp.where(causal, sp, 0.0)
        csum = jnp.dot(tri, sp.astype(BF16), preferred_element_type=F32)
        w = jnp.exp2(z - carry - csum)
        if masked_from is not None:
            w = jnp.where(causal, w, 0.0)
        acc = acc + jnp.dot(vt_ref[0, 0, j], w.astype(BF16), preferred_element_type=F32)
        return carry + csum[0:1, :], acc

    carry = jnp.zeros((1, bq), F32)
    acc = jnp.zeros((HEAD_DIM, bq), F32)
    for d in range(n_diag - 1, -1, -1):
        carry, acc = block(qi * n_diag + d, carry, acc, d)

    def body(n, state):
        return block(qi * n_diag - 1 - n, state[0], state[1], None)

    carry, acc = lax.fori_loop(0, qi * n_diag, body, (carry, acc))
    o_ref[0, 0] = acc.astype(BF16)


def _attn_call(qt, k, vt):
    bsz, _, _, seq = qt.shape
    bq = QUERY_BLOCK
    return pl.pallas_call(
        _attn_kernel,
        out_shape=jax.ShapeDtypeStruct((bsz, N_HEADS, HEAD_DIM, seq), BF16),
        grid=(bsz, N_HEADS, seq // bq),
        in_specs=[
            pl.BlockSpec((1, 1, HEAD_PAD, bq), lambda b, h, i: (b, h, 0, i)),
            pl.BlockSpec((1, seq, HEAD_PAD), lambda b, h, i: (b, 0, h // 2)),
            pl.BlockSpec((1, 1, seq // KEY_BLOCK, HEAD_DIM, KEY_BLOCK),
                         lambda b, h, i: (b, h, 0, 0, 0)),
        ],
        out_specs=pl.BlockSpec((1, 1, HEAD_DIM, bq), lambda b, h, i: (b, h, 0, i)),
        compiler_params=pltpu.CompilerParams(
            dimension_semantics=("arbitrary", "arbitrary", "arbitrary"),
            vmem_limit_bytes=VMEM_LIMIT_BYTES),
        name="stickbreak_attn",
    )(qt, k, vt)


def _post_kernel(x_ref, ot_ref, gate_ref, gconv_ref, p_ref,
                 wao_ref, wo_ref, wup_ref, wdn_ref, wpg_ref, wpp_ref,
                 g1_ref, g2_ref, g3_ref, g4_ref, out_ref):
    tm = x_ref.shape[1]
    y_attn = lax.dot_general(ot_ref[0], wao_ref[...], (((0,), (0,)), ((), ())),
                             preferred_element_type=F32)
    mix_in = gate_ref[0] * y_attn + gconv_ref[0]
    mixed = jnp.dot(mix_in.astype(BF16), wo_ref[...], preferred_element_type=F32)
    x1 = x_ref[0] + _rms_norm(mixed, g1_ref[...])

    h2 = _rms_norm(x1, g2_ref[...]).astype(BF16)
    f = jnp.zeros((tm, D_MODEL), F32)
    for c in range(D_FF // FF_CHUNK):
        up = jnp.dot(h2, wup_ref[:, c * FF_CHUNK:(c + 1) * FF_CHUNK],
                     preferred_element_type=F32)
        act = jnp.square(jnp.maximum(up, 0.0)).astype(BF16)
        f = f + jnp.dot(act, wdn_ref[c * FF_CHUNK:(c + 1) * FF_CHUNK, :],
                        preferred_element_type=F32)
    x2 = x1 + _rms_norm(f, g3_ref[...])

    h3 = _rms_norm(x2, g4_ref[...]).astype(BF16)
    ple_gate = _sigmoid(jnp.dot(h3, wpg_ref[...], preferred_element_type=F32))
    ple = jnp.dot(p_ref[0].astype(BF16), wpp_ref[...], preferred_element_type=F32)
    out_ref[0] = x2 + ple_gate * ple


def _post_call(x, ot, gate, gconv, p, w_ao, w_o, w_up, w_dn, w_pg, w_pp, g1, g2, g3, g4):
    bsz, seq, _ = x.shape
    tm = TOKEN_TILE
    tok = lambda width: pl.BlockSpec((1, tm, width), lambda b, i: (b, i, 0))
    return pl.pallas_call(
        _post_kernel,
        out_shape=jax.ShapeDtypeStruct((bsz, seq, D_MODEL), F32),
        grid=(bsz, seq // tm),
        in_specs=[
            tok(D_MODEL),
            pl.BlockSpec((1, ATTN_WIDTH, tm), lambda b, i: (b, 0, i)),
            tok(D_MODEL),
            tok(D_MODEL),
            tok(PLE_DIM),
            _resident((ATTN_WIDTH, D_MODEL)),
            _resident((D_MODEL, D_MODEL)),
            _resident((D_MODEL, D_FF)),
            _resident((D_FF, D_MODEL)),
            _resident((D_MODEL, D_MODEL)),
            _resident((PLE_DIM, D_MODEL)),
            _resident((1, D_MODEL)),
            _resident((1, D_MODEL)),
            _resident((1, D_MODEL)),
            _resident((1, D_MODEL)),
        ],
        out_specs=tok(D_MODEL),
        compiler_params=pltpu.CompilerParams(
            dimension_semantics=("arbitrary", "arbitrary"),
            vmem_limit_bytes=VMEM_LIMIT_BYTES),
        name="post_mix_mlp",
    )(x, ot, gate, gconv, p, w_ao, w_o, w_up, w_dn, w_pg, w_pp, g1, g2, g3, g4)


def kernel(x, p, g_pre_mix, w_in, b_gate, w_conv, w_attn_out, w_conv_out, w_o,
           g_post_mix, g_pre_mlp, w_up, w_down, g_post_mlp, g_ple, w_ple_gate, w_ple_proj):
    bsz, seq, _ = x.shape
    depth = w_in.shape[0]
    row = lambda v: v.reshape(1, -1)
    for i in range(depth):
        w = w_in[i].astype(BF16)
        w_t = jnp.concatenate([w[:, 0:ATTN_WIDTH], w[:, 2 * ATTN_WIDTH:3 * ATTN_WIDTH]], axis=1).T
        w_n = jnp.concatenate([w[:, ATTN_WIDTH:2 * ATTN_WIDTH], w[:, 3 * ATTN_WIDTH:]], axis=1)
        qt, k, vt, gate, gconv = _pre_call(
            x, row(g_pre_mix[i]), w_t, w_n, row(b_gate[i]), w_conv[i],
            w_conv_out[i].astype(BF16))
        ot = _attn_call(qt, k, vt).reshape(bsz, ATTN_WIDTH, seq)
        x = _post_call(
            x, ot, gate, gconv, p[i],
            w_attn_out[i].astype(BF16), w_o[i].astype(BF16), w_up[i].astype(BF16),
            w_down[i].astype(BF16), w_ple_gate[i].astype(BF16), w_ple_proj[i].astype(BF16),
            row(g_post_mix[i]), row(g_pre_mlp[i]), row(g_post_mlp[i]), row(g_ple[i]))
    return x
```

```python
import functools
import math

import jax
import jax.numpy as jnp
from jax import lax
from jax.experimental import pallas as pl
from jax.experimental.pallas import tpu as pltpu

D_MODEL = 1024
N_HEADS = 8
HEAD_DIM = 64
ATTN_WIDTH = N_HEADS * HEAD_DIM
CONV_WIDTH = 512
CONV_K = 3
D_FF = 4 * D_MODEL
PLE_DIM = 256
EPS = 1e-6

LANES = 128
SUBLANES = 8
VMEM_LIMIT_BYTES = 56 * 1024 * 1024

TOKEN_TILE = 512
KEY_BLOCK = 256
QUERY_BLOCK = 512
FF_CHUNK = 1024
HEADS_PER_STEP = 2
HEAD_PAD = HEADS_PER_STEP * HEAD_DIM

BF16 = jnp.bfloat16
F32 = jnp.float32


def _rms_norm(xf, g):
    var = jnp.mean(xf * xf, axis=-1, keepdims=True)
    return xf * lax.rsqrt(var + EPS) * g


def _sigmoid(v):
    return 1.0 / (1.0 + jnp.exp(-v))


def _resident(shape):
    zeros = (0,) * len(shape)
    return pl.BlockSpec(shape, lambda *_: zeros, pipeline_mode=pl.Buffered(1))


def _pre_kernel(x_ref, g_ref, wt_ref, wn_ref, bg_ref, wc_ref, wco_ref,
                qt_ref, k_ref, vt_ref, gate_ref, gconv_ref, ubuf_ref, *, q_scale):
    tm = x_ref.shape[1]
    h = _rms_norm(x_ref[0], g_ref[...]).astype(BF16)

    tr = lax.dot_general(wt_ref[...], h, (((1,), (1,)), ((), ())),
                         preferred_element_type=F32)
    zeros_half = jnp.zeros((HEAD_DIM, tm), BF16)
    for hd in range(N_HEADS):
        q_h = (tr[hd * HEAD_DIM:(hd + 1) * HEAD_DIM] * q_scale).astype(BF16)
        lo = (hd % 2) * HEAD_DIM
        qt_ref[0, hd, lo:lo + HEAD_DIM, :] = q_h
        qt_ref[0, hd, HEAD_DIM - lo:2 * HEAD_DIM - lo, :] = zeros_half
        v_h = tr[ATTN_WIDTH + hd * HEAD_DIM:ATTN_WIDTH + (hd + 1) * HEAD_DIM].astype(BF16)
        for c in range(tm // KEY_BLOCK):
            vt_ref[0, hd, c] = v_h[:, c * KEY_BLOCK:(c + 1) * KEY_BLOCK]

    k_ref[0] = jnp.dot(h, wn_ref[:, 0:ATTN_WIDTH],
                       preferred_element_type=F32).astype(BF16)

    c0 = ATTN_WIDTH
    conv3 = jnp.dot(h, wn_ref[:, c0:c0 + 3 * CONV_WIDTH], preferred_element_type=F32)
    cb = conv3[:, 0:CONV_WIDTH]
    u = conv3[:, CONV_WIDTH:2 * CONV_WIDTH] * conv3[:, 2 * CONV_WIDTH:3 * CONV_WIDTH]

    @pl.when(pl.program_id(1) == 0)
    def _():
        ubuf_ref[0:SUBLANES, :] = jnp.zeros((SUBLANES, CONV_WIDTH), F32)

    ubuf_ref[SUBLANES:SUBLANES + tm, :] = u
    u1 = ubuf_ref[SUBLANES - 1:SUBLANES - 1 + tm, :]
    u2 = ubuf_ref[SUBLANES - 2:SUBLANES - 2 + tm, :]
    ubuf_ref[0:SUBLANES, :] = u[tm - SUBLANES:tm, :]
    wc = wc_ref[...]
    conv = wc[0:1, :] * u2 + wc[1:2, :] * u1 + wc[2:3, :] * u
    y_conv = jnp.dot((cb * conv).astype(BF16), wco_ref[...],
                     preferred_element_type=F32)

    g0 = ATTN_WIDTH + 3 * CONV_WIDTH
    ga = jnp.dot(h, wn_ref[:, g0:g0 + D_MODEL], preferred_element_type=F32)
    gate_ref[0] = _sigmoid(ga + bg_ref[:, 0:D_MODEL])
    gc = jnp.dot(h, wn_ref[:, g0 + D_MODEL:g0 + 2 * D_MODEL], preferred_element_type=F32)
    gconv_ref[0] = _sigmoid(gc + bg_ref[:, D_MODEL:2 * D_MODEL]) * y_conv


def _pre_call(x, g, w_t, w_n, b_gate, w_conv, w_conv_out):
    bsz, seq, _ = x.shape
    tm = TOKEN_TILE
    q_scale = (HEAD_DIM ** -0.5) * math.log2(math.e)
    n_nat = w_n.shape[1]
    out_shape = (
        jax.ShapeDtypeStruct((bsz, N_HEADS, HEAD_PAD, seq), BF16),
        jax.ShapeDtypeStruct((bsz, seq, ATTN_WIDTH), BF16),
        jax.ShapeDtypeStruct((bsz, N_HEADS, seq // KEY_BLOCK, HEAD_DIM, KEY_BLOCK), BF16),
        jax.ShapeDtypeStruct((bsz, seq, D_MODEL), F32),
        jax.ShapeDtypeStruct((bsz, seq, D_MODEL), F32),
    )
    return pl.pallas_call(
        functools.partial(_pre_kernel, q_scale=q_scale),
        out_shape=out_shape,
        grid=(bsz, seq // tm),
        in_specs=[
            pl.BlockSpec((1, tm, D_MODEL), lambda b, i: (b, i, 0)),
            _resident((1, D_MODEL)),
            _resident((2 * ATTN_WIDTH, D_MODEL)),
            _resident((D_MODEL, n_nat)),
            _resident((1, 2 * D_MODEL)),
            _resident((CONV_K, CONV_WIDTH)),
            _resident((CONV_WIDTH, D_MODEL)),
        ],
        out_specs=(
            pl.BlockSpec((1, N_HEADS, HEAD_PAD, tm), lambda b, i: (b, 0, 0, i)),
            pl.BlockSpec((1, tm, ATTN_WIDTH), lambda b, i: (b, i, 0)),
            pl.BlockSpec((1, N_HEADS, tm // KEY_BLOCK, HEAD_DIM, KEY_BLOCK),
                         lambda b, i: (b, 0, i, 0, 0)),
            pl.BlockSpec((1, tm, D_MODEL), lambda b, i: (b, i, 0)),
            pl.BlockSpec((1, tm, D_MODEL), lambda b, i: (b, i, 0)),
        ),
        scratch_shapes=[pltpu.VMEM((tm + SUBLANES, CONV_WIDTH), F32)],
        compiler_params=pltpu.CompilerParams(
            dimension_semantics=("arbitrary", "arbitrary"),
            vmem_limit_bytes=VMEM_LIMIT_BYTES),
        name="pre_proj",
    )(x, g, w_t, w_n, b_gate, w_conv, w_conv_out)


def _softplus2(z):
    return jnp.maximum(z, 0.0) + jnp.log2(1.0 + jnp.exp2(-jnp.abs(z)))


def _attn_kernel(qt_ref, k_ref, vt_ref, o_ref, z_ref, d_ref, tot_ref, carry_ref, acc_ref):
    bq = qt_ref.shape[3]
    bk = KEY_BLOCK
    qi = pl.program_id(2)
    n_diag = bq // bk
    assert n_diag == 2
    heads = range(HEADS_PER_STEP)
    top = qi * n_diag + n_diag - 1

    row = lax.broadcasted_iota(jnp.int32, (bk, bk), 0)
    col = lax.broadcasted_iota(jnp.int32, (bk, bk), 1)
    tri = (col >= row).astype(BF16)

    def scores(m, slot, diag=None):
        start = pl.multiple_of((top - m) * bk, bk)
        kb = k_ref[0, pl.ds(start, bk), :]
        for hd in heads:
            z = jnp.dot(kb, qt_ref[0, hd], preferred_element_type=F32)
            if diag is not None:
                key_pos = lax.broadcasted_iota(jnp.int32, (bk, bq), 0) + diag * bk
                q_pos = lax.broadcasted_iota(jnp.int32, (bk, bq), 1)
                z = jnp.where(key_pos < q_pos, z, -jnp.inf)
            z_ref[slot, hd] = z

    def cumsum(slot):
        for hd in heads:
            z = z_ref[slot, hd]
            csum = jnp.dot(tri, _softplus2(z).astype(BF16), preferred_element_type=F32)
            d_ref[slot, hd] = z - csum
            tot_ref[slot, hd] = csum[0:1, :]

    def weights(m, slot):
        for hd in heads:
            w = jnp.exp2(d_ref[slot, hd] - carry_ref[hd])
            carry_ref[hd] = carry_ref[hd] + tot_ref[slot, hd]
            acc_ref[hd] += jnp.dot(vt_ref[0, hd, top - m], w.astype(BF16),
                                   preferred_element_type=F32)

    carry_ref[...] = jnp.zeros_like(carry_ref)
    acc_ref[...] = jnp.zeros_like(acc_ref)
    scores(0, 0, diag=1)
    scores(1, 1, diag=0)
    cumsum(0)

    def body(i, _):
        t = 2 * i
        scores(t + 2, 0)
        cumsum(1)
        weights(t, 0)
        scores(t + 3, 1)
        cumsum(0)
        weights(t + 1, 1)
        return 0

    lax.fori_loop(0, qi, body, 0)
    last = n_diag * qi
    cumsum(1)
    weights(last, 0)
    weights(last + 1, 1)
    for hd in heads:
        o_ref[0, hd] = acc_ref[hd].astype(BF16)


def _attn_call(qt, k, vt):
    bsz, _, _, seq = qt.shape
    bq = QUERY_BLOCK
    hp = HEADS_PER_STEP
    return pl.pallas_call(
        _attn_kernel,
        out_shape=jax.ShapeDtypeStruct((bsz, N_HEADS, HEAD_DIM, seq), BF16),
        grid=(bsz, N_HEADS // hp, seq // bq),
        in_specs=[
            pl.BlockSpec((1, hp, HEAD_PAD, bq), lambda b, h, i: (b, h, 0, i)),
            pl.BlockSpec((1, seq, HEAD_PAD), lambda b, h, i: (b, 0, h)),
            pl.BlockSpec((1, hp, seq // KEY_BLOCK, HEAD_DIM, KEY_BLOCK),
                         lambda b, h, i: (b, h, 0, 0, 0)),
        ],
        out_specs=pl.BlockSpec((1, hp, HEAD_DIM, bq), lambda b, h, i: (b, h, 0, i)),
        scratch_shapes=[
            pltpu.VMEM((2, hp, KEY_BLOCK, bq), F32),
            pltpu.VMEM((2, hp, KEY_BLOCK, bq), F32),
            pltpu.VMEM((2, hp, 1, bq), F32),
            pltpu.VMEM((hp, 1, bq), F32),
            pltpu.VMEM((hp, HEAD_DIM, bq), F32),
        ],
        compiler_params=pltpu.CompilerParams(
            dimension_semantics=("arbitrary", "arbitrary", "arbitrary"),
            vmem_limit_bytes=VMEM_LIMIT_BYTES),
        name="stickbreak_attn",
    )(qt, k, vt)


def _post_kernel(x_ref, ot_ref, gate_ref, gconv_ref, p_ref,
                 wao_ref, wo_ref, wup_ref, wdn_ref, wpg_ref, wpp_ref,
                 g1_ref, g2_ref, g3_ref, g4_ref, out_ref):
    tm = x_ref.shape[1]
    y_attn = lax.dot_general(ot_ref[0], wao_ref[...], (((0,), (0,)), ((), ())),
                             preferred_element_type=F32)
    mix_in = gate_ref[0] * y_attn + gconv_ref[0]
    mixed = jnp.dot(mix_in.astype(BF16), wo_ref[...], preferred_element_type=F32)
    x1 = x_ref[0] + _rms_norm(mixed, g1_ref[...])

    h2 = _rms_norm(x1, g2_ref[...]).astype(BF16)
    f = jnp.zeros((tm, D_MODEL), F32)
    for c in range(D_FF // FF_CHUNK):
        up = jnp.dot(h2, wup_ref[:, c * FF_CHUNK:(c + 1) * FF_CHUNK],
                     preferred_element_type=F32)
        act = jnp.square(jnp.maximum(up, 0.0)).astype(BF16)
        f = f + jnp.dot(act, wdn_ref[c * FF_CHUNK:(c + 1) * FF_CHUNK, :],
                        preferred_element_type=F32)
    x2 = x1 + _rms_norm(f, g3_ref[...])

    h3 = _rms_norm(x2, g4_ref[...]).astype(BF16)
    ple_gate = _sigmoid(jnp.dot(h3, wpg_ref[...], preferred_element_type=F32))
    ple = jnp.dot(p_ref[0].astype(BF16), wpp_ref[...], preferred_element_type=F32)
    out_ref[0] = x2 + ple_gate * ple


def _post_call(x, ot, gate, gconv, p, w_ao, w_o, w_up, w_dn, w_pg, w_pp, g1, g2, g3, g4):
    bsz, seq, _ = x.shape
    tm = TOKEN_TILE
    tok = lambda width: pl.BlockSpec((1, tm, width), lambda b, i: (b, i, 0))
    return pl.pallas_call(
        _post_kernel,
        out_shape=jax.ShapeDtypeStruct((bsz, seq, D_MODEL), F32),
        grid=(bsz, seq // tm),
        in_specs=[
            tok(D_MODEL),
            pl.BlockSpec((1, ATTN_WIDTH, tm), lambda b, i: (b, 0, i)),
            tok(D_MODEL),
            tok(D_MODEL),
            tok(PLE_DIM),
            _resident((ATTN_WIDTH, D_MODEL)),
            _resident((D_MODEL, D_MODEL)),
            _resident((D_MODEL, D_FF)),
            _resident((D_FF, D_MODEL)),
            _resident((D_MODEL, D_MODEL)),
            _resident((PLE_DIM, D_MODEL)),
            _resident((1, D_MODEL)),
            _resident((1, D_MODEL)),
            _resident((1, D_MODEL)),
            _resident((1, D_MODEL)),
        ],
        out_specs=tok(D_MODEL),
        compiler_params=pltpu.CompilerParams(
            dimension_semantics=("arbitrary", "arbitrary"),
            vmem_limit_bytes=VMEM_LIMIT_BYTES),
        name="post_mix_mlp",
    )(x, ot, gate, gconv, p, w_ao, w_o, w_up, w_dn, w_pg, w_pp, g1, g2, g3, g4)


def kernel(x, p, g_pre_mix, w_in, b_gate, w_conv, w_attn_out, w_conv_out, w_o,
           g_post_mix, g_pre_mlp, w_up, w_down, g_post_mlp, g_ple, w_ple_gate, w_ple_proj):
    bsz, seq, _ = x.shape
    depth = w_in.shape[0]
    row = lambda v: v.reshape(1, -1)
    for i in range(depth):
        w = w_in[i].astype(BF16)
        w_t = jnp.concatenate([w[:, 0:ATTN_WIDTH], w[:, 2 * ATTN_WIDTH:3 * ATTN_WIDTH]], axis=1).T
        w_n = jnp.concatenate([w[:, ATTN_WIDTH:2 * ATTN_WIDTH], w[:, 3 * ATTN_WIDTH:]], axis=1)
        qt, k, vt, gate, gconv = _pre_call(
            x, row(g_pre_mix[i]), w_t, w_n, row(b_gate[i]), w_conv[i],
            w_conv_out[i].astype(BF16))
        ot = _attn_call(qt, k, vt).reshape(bsz, ATTN_WIDTH, seq)
        x = _post_call(
            x, ot, gate, gconv, p[i],
            w_attn_out[i].astype(BF16), w_o[i].astype(BF16), w_up[i].astype(BF16),
            w_down[i].astype(BF16), w_ple_gate[i].astype(BF16), w_ple_proj[i].astype(BF16),
            row(g_post_mix[i]), row(g_pre_mlp[i]), row(g_post_mlp[i]), row(g_ple[i]))
    return x
```

```python
import functools
import math

import jax
import jax.numpy as jnp
from jax import lax
from jax.experimental import pallas as pl
from jax.experimental.pallas import tpu as pltpu

D_MODEL = 1024
N_HEADS = 8
HEAD_DIM = 64
ATTN_WIDTH = N_HEADS * HEAD_DIM
CONV_WIDTH = 512
CONV_K = 3
D_FF = 4 * D_MODEL
PLE_DIM = 256
EPS = 1e-6

LANES = 128
SUBLANES = 8
VMEM_LIMIT_BYTES = 56 * 1024 * 1024

TOKEN_TILE = 512
KEY_BLOCK = 256
QUERY_BLOCK = 512
FF_CHUNK = 1024
HEAD_PAD = 2 * HEAD_DIM
HEADS_PER_STEP = 2
EXP2_UNDERFLOW = 160.0

BF16 = jnp.bfloat16
F32 = jnp.float32


def _rms_norm(xf, g):
    var = jnp.mean(xf * xf, axis=-1, keepdims=True)
    return xf * lax.rsqrt(var + EPS) * g


def _sigmoid(v):
    return 1.0 / (1.0 + jnp.exp(-v))


def _resident(shape):
    zeros = (0,) * len(shape)
    return pl.BlockSpec(shape, lambda *_: zeros, pipeline_mode=pl.Buffered(1))


def _pre_kernel(x_ref, g_ref, wt_ref, wn_ref, bg_ref, wc_ref, wco_ref,
                qt_ref, k_ref, vt_ref, gate_ref, gconv_ref, ubuf_ref, *, q_scale):
    tm = x_ref.shape[1]
    h = _rms_norm(x_ref[0], g_ref[...]).astype(BF16)

    tr = lax.dot_general(wt_ref[...], h, (((1,), (1,)), ((), ())),
                         preferred_element_type=F32)
    zeros_half = jnp.zeros((HEAD_DIM, tm), BF16)
    for hd in range(N_HEADS):
        q_h = (tr[hd * HEAD_DIM:(hd + 1) * HEAD_DIM] * q_scale).astype(BF16)
        lo = (hd % 2) * HEAD_DIM
        qt_ref[0, hd, lo:lo + HEAD_DIM, :] = q_h
        qt_ref[0, hd, HEAD_DIM - lo:2 * HEAD_DIM - lo, :] = zeros_half
        v_h = tr[ATTN_WIDTH + hd * HEAD_DIM:ATTN_WIDTH + (hd + 1) * HEAD_DIM].astype(BF16)
        for c in range(tm // KEY_BLOCK):
            vt_ref[0, hd, c] = v_h[:, c * KEY_BLOCK:(c + 1) * KEY_BLOCK]

    k_ref[0] = jnp.dot(h, wn_ref[:, 0:ATTN_WIDTH],
                       preferred_element_type=F32).astype(BF16)

    c0 = ATTN_WIDTH
    conv3 = jnp.dot(h, wn_ref[:, c0:c0 + 3 * CONV_WIDTH], preferred_element_type=F32)
    cb = conv3[:, 0:CONV_WIDTH]
    u = conv3[:, CONV_WIDTH:2 * CONV_WIDTH] * conv3[:, 2 * CONV_WIDTH:3 * CONV_WIDTH]

    @pl.when(pl.program_id(1) == 0)
    def _():
        ubuf_ref[0:SUBLANES, :] = jnp.zeros((SUBLANES, CONV_WIDTH), F32)

    ubuf_ref[SUBLANES:SUBLANES + tm, :] = u
    u1 = ubuf_ref[SUBLANES - 1:SUBLANES - 1 + tm, :]
    u2 = ubuf_ref[SUBLANES - 2:SUBLANES - 2 + tm, :]
    ubuf_ref[0:SUBLANES, :] = u[tm - SUBLANES:tm, :]
    wc = wc_ref[...]
    conv = wc[0:1, :] * u2 + wc[1:2, :] * u1 + wc[2:3, :] * u
    y_conv = jnp.dot((cb * conv).astype(BF16), wco_ref[...],
                     preferred_element_type=F32)

    g0 = ATTN_WIDTH + 3 * CONV_WIDTH
    ga = jnp.dot(h, wn_ref[:, g0:g0 + D_MODEL], preferred_element_type=F32)
    gate_ref[0] = _sigmoid(ga + bg_ref[:, 0:D_MODEL])
    gc = jnp.dot(h, wn_ref[:, g0 + D_MODEL:g0 + 2 * D_MODEL], preferred_element_type=F32)
    gconv_ref[0] = _sigmoid(gc + bg_ref[:, D_MODEL:2 * D_MODEL]) * y_conv


def _pre_call(x, g, w_t, w_n, b_gate, w_conv, w_conv_out):
    bsz, seq, _ = x.shape
    tm = TOKEN_TILE
    q_scale = (HEAD_DIM ** -0.5) * math.log2(math.e)
    n_nat = w_n.shape[1]
    out_shape = (
        jax.ShapeDtypeStruct((bsz, N_HEADS, HEAD_PAD, seq), BF16),
        jax.ShapeDtypeStruct((bsz, seq, ATTN_WIDTH), BF16),
        jax.ShapeDtypeStruct((bsz, N_HEADS, seq // KEY_BLOCK, HEAD_DIM, KEY_BLOCK), BF16),
        jax.ShapeDtypeStruct((bsz, seq, D_MODEL), F32),
        jax.ShapeDtypeStruct((bsz, seq, D_MODEL), F32),
    )
    return pl.pallas_call(
        functools.partial(_pre_kernel, q_scale=q_scale),
        out_shape=out_shape,
        grid=(bsz, seq // tm),
        in_specs=[
            pl.BlockSpec((1, tm, D_MODEL), lambda b, i: (b, i, 0)),
            _resident((1, D_MODEL)),
            _resident((2 * ATTN_WIDTH, D_MODEL)),
            _resident((D_MODEL, n_nat)),
            _resident((1, 2 * D_MODEL)),
            _resident((CONV_K, CONV_WIDTH)),
            _resident((CONV_WIDTH, D_MODEL)),
        ],
        out_specs=(
            pl.BlockSpec((1, N_HEADS, HEAD_PAD, tm), lambda b, i: (b, 0, 0, i)),
            pl.BlockSpec((1, tm, ATTN_WIDTH), lambda b, i: (b, i, 0)),
            pl.BlockSpec((1, N_HEADS, tm // KEY_BLOCK, HEAD_DIM, KEY_BLOCK),
                         lambda b, i: (b, 0, i, 0, 0)),
            pl.BlockSpec((1, tm, D_MODEL), lambda b, i: (b, i, 0)),
            pl.BlockSpec((1, tm, D_MODEL), lambda b, i: (b, i, 0)),
        ),
        scratch_shapes=[pltpu.VMEM((tm + SUBLANES, CONV_WIDTH), F32)],
        compiler_params=pltpu.CompilerParams(
            dimension_semantics=("arbitrary", "arbitrary"),
            vmem_limit_bytes=VMEM_LIMIT_BYTES),
        name="pre_proj",
    )(x, g, w_t, w_n, b_gate, w_conv, w_conv_out)


def _softplus2(z):
    return jnp.maximum(z, 0) + jnp.log2(1 + jnp.exp2(-jnp.abs(z)))


def _attn_kernel(qt_ref, k_ref, vt_ref, o_ref, z_ref, d_ref, tot_ref, carry_ref, acc_ref):
    bq = qt_ref.shape[3]
    bk = KEY_BLOCK
    qi = pl.program_id(2)
    assert bq == 2 * bk
    heads = range(HEADS_PER_STEP)
    top = 2 * qi + 1

    row = lax.broadcasted_iota(jnp.int32, (bk, bk), 0)
    col = lax.broadcasted_iota(jnp.int32, (bk, bk), 1)
    neg_tri = jnp.where(col >= row, -1.0, 0.0).astype(BF16)

    def scores(m, par, diag=None):
        start = pl.multiple_of((top - m) * bk, bk)
        for hd in heads:
            pair = (hd // 2) * HEAD_PAD
            kb = k_ref[0, pl.ds(start, bk), pair:pair + HEAD_PAD]
            z = jnp.dot(kb, qt_ref[0, hd], preferred_element_type=F32)
            if diag is not None:
                key_pos = lax.broadcasted_iota(jnp.int32, (bk, bq), 0) + diag * bk
                q_pos = lax.broadcasted_iota(jnp.int32, (bk, bq), 1)
                z = jnp.where(key_pos < q_pos, z, -jnp.inf)
            z_ref[par, hd] = z

    def cumsum(par):
        for hd in heads:
            z = z_ref[par, hd]
            ncsum = jnp.dot(neg_tri, _softplus2(z).astype(BF16), preferred_element_type=F32)
            d_ref[par, hd] = ncsum + z
            tot_ref[par, hd] = ncsum[0:1, :]

    def weights(m, par):
        for hd in heads:
            pv = jnp.dot(vt_ref[0, hd, top - m], jnp.exp2(d_ref[par, hd]).astype(BF16),
                         preferred_element_type=F32)
            ncarry = carry_ref[hd]
            acc_ref[hd] += pv * jnp.exp2(ncarry)
            carry_ref[hd] = ncarry + tot_ref[par, hd]

    carry_ref[...] = jnp.zeros_like(carry_ref)
    acc_ref[...] = jnp.zeros_like(acc_ref)
    scores(0, 0, diag=1)
    scores(1, 1, diag=0)
    cumsum(0)

    def cond(state):
        i, live = state
        return jnp.logical_and(i < qi, live)

    def body(state):
        i, _ = state
        t = 2 * i
        scores(t + 2, 0)
        cumsum(1)
        weights(t, 0)
        scores(t + 3, 1)
        cumsum(0)
        weights(t + 1, 1)
        return i + 1, jnp.max(carry_ref[...]) >= -EXP2_UNDERFLOW

    done, _ = lax.while_loop(cond, body, (jnp.int32(0), True))
    cumsum(1)
    weights(2 * done, 0)
    weights(2 * done + 1, 1)
    for hd in heads:
        o_ref[0, hd] = acc_ref[hd].astype(BF16)


def _attn_call(qt, k, vt):
    bsz, _, _, seq = qt.shape
    bq = QUERY_BLOCK
    hp = HEADS_PER_STEP
    return pl.pallas_call(
        _attn_kernel,
        out_shape=jax.ShapeDtypeStruct((bsz, N_HEADS, HEAD_DIM, seq), BF16),
        grid=(bsz, N_HEADS // hp, seq // bq),
        in_specs=[
            pl.BlockSpec((1, hp, HEAD_PAD, bq), lambda b, h, i: (b, h, 0, i)),
            pl.BlockSpec((1, seq, hp * HEAD_DIM), lambda b, h, i: (b, 0, h)),
            pl.BlockSpec((1, hp, seq // KEY_BLOCK, HEAD_DIM, KEY_BLOCK),
                         lambda b, h, i: (b, h, 0, 0, 0)),
        ],
        out_specs=pl.BlockSpec((1, hp, HEAD_DIM, bq), lambda b, h, i: (b, h, 0, i)),
        scratch_shapes=[
            pltpu.VMEM((2, hp, KEY_BLOCK, bq), F32),
            pltpu.VMEM((2, hp, KEY_BLOCK, bq), F32),
            pltpu.VMEM((2, hp, 1, bq), F32),
            pltpu.VMEM((hp, 1, bq), F32),
            pltpu.VMEM((hp, HEAD_DIM, bq), F32),
        ],
        compiler_params=pltpu.CompilerParams(
            dimension_semantics=("arbitrary", "arbitrary", "arbitrary"),
            vmem_limit_bytes=VMEM_LIMIT_BYTES),
        name="stickbreak_attn",
    )(qt, k, vt)


def _post_kernel(x_ref, ot_ref, gate_ref, gconv_ref, p_ref,
                 wao_ref, wo_ref, wup_ref, wdn_ref, wpg_ref, wpp_ref,
                 g1_ref, g2_ref, g3_ref, g4_ref, out_ref):
    tm = x_ref.shape[1]
    y_attn = lax.dot_general(ot_ref[0], wao_ref[...], (((0,), (0,)), ((), ())),
                             preferred_element_type=F32)
    mix_in = gate_ref[0] * y_attn + gconv_ref[0]
    mixed = jnp.dot(mix_in.astype(BF16), wo_ref[...], preferred_element_type=F32)
    x1 = x_ref[0] + _rms_norm(mixed, g1_ref[...])

    h2 = _rms_norm(x1, g2_ref[...]).astype(BF16)
    f = jnp.zeros((tm, D_MODEL), F32)
    for c in range(D_FF // FF_CHUNK):
        up = jnp.dot(h2, wup_ref[:, c * FF_CHUNK:(c + 1) * FF_CHUNK],
                     preferred_element_type=F32)
        act = jnp.square(jnp.maximum(up, 0.0)).astype(BF16)
        f = f + jnp.dot(act, wdn_ref[c * FF_CHUNK:(c + 1) * FF_CHUNK, :],
                        preferred_element_type=F32)
    x2 = x1 + _rms_norm(f, g3_ref[...])

    h3 = _rms_norm(x2, g4_ref[...]).astype(BF16)
    ple_gate = _sigmoid(jnp.dot(h3, wpg_ref[...], preferred_element_type=F32))
    ple = jnp.dot(p_ref[0].astype(BF16), wpp_ref[...], preferred_element_type=F32)
    out_ref[0] = x2 + ple_gate * ple


def _post_call(x, ot, gate, gconv, p, w_ao, w_o, w_up, w_dn, w_pg, w_pp, g1, g2, g3, g4):
    bsz, seq, _ = x.shape
    tm = TOKEN_TILE
    tok = lambda width: pl.BlockSpec((1, tm, width), lambda b, i: (b, i, 0))
    return pl.pallas_call(
        _post_kernel,
        out_shape=jax.ShapeDtypeStruct((bsz, seq, D_MODEL), F32),
        grid=(bsz, seq // tm),
        in_specs=[
            tok(D_MODEL),
            pl.BlockSpec((1, ATTN_WIDTH, tm), lambda b, i: (b, 0, i)),
            tok(D_MODEL),
            tok(D_MODEL),
            tok(PLE_DIM),
            _resident((ATTN_WIDTH, D_MODEL)),
            _resident((D_MODEL, D_MODEL)),
            _resident((D_MODEL, D_FF)),
            _resident((D_FF, D_MODEL)),
            _resident((D_MODEL, D_MODEL)),
            _resident((PLE_DIM, D_MODEL)),
            _resident((1, D_MODEL)),
            _resident((1, D_MODEL)),
            _resident((1, D_MODEL)),
            _resident((1, D_MODEL)),
        ],
        out_specs=tok(D_MODEL),
        compiler_params=pltpu.CompilerParams(
            dimension_semantics=("arbitrary", "arbitrary"),
            vmem_limit_bytes=VMEM_LIMIT_BYTES),
        name="post_mix_mlp",
    )(x, ot, gate, gconv, p, w_ao, w_o, w_up, w_dn, w_pg, w_pp, g1, g2, g3, g4)


def kernel(x, p, g_pre_mix, w_in, b_gate, w_conv, w_attn_out, w_conv_out, w_o,
           g_post_mix, g_pre_mlp, w_up, w_down, g_post_mlp, g_ple, w_ple_gate, w_ple_proj):
    bsz, seq, _ = x.shape
    depth = w_in.shape[0]
    row = lambda v: v.reshape(1, -1)
    for i in range(depth):
        w = w_in[i].astype(BF16)
        w_t = jnp.concatenate([w[:, 0:ATTN_WIDTH], w[:, 2 * ATTN_WIDTH:3 * ATTN_WIDTH]], axis=1).T
        w_n = jnp.concatenate([w[:, ATTN_WIDTH:2 * ATTN_WIDTH], w[:, 3 * ATTN_WIDTH:]], axis=1)
        qt, k, vt, gate, gconv = _pre_call(
            x, row(g_pre_mix[i]), w_t, w_n, row(b_gate[i]), w_conv[i],
            w_conv_out[i].astype(BF16))
        ot = _attn_call(qt, k, vt).reshape(bsz, ATTN_WIDTH, seq)
        x = _post_call(
            x, ot, gate, gconv, p[i],
            w_attn_out[i].astype(BF16), w_o[i].astype(BF16), w_up[i].astype(BF16),
            w_down[i].astype(BF16), w_ple_gate[i].astype(BF16), w_ple_proj[i].astype(BF16),
            row(g_post_mix[i]), row(g_pre_mlp[i]), row(g_post_mlp[i]), row(g_ple[i]))
    return x
```

```python
import functools
import math

import jax
import jax.numpy as jnp
from jax import lax
from jax.experimental import pallas as pl
from jax.experimental.pallas import tpu as pltpu

D_MODEL = 1024
N_HEADS = 8
HEAD_DIM = 64
ATTN_WIDTH = N_HEADS * HEAD_DIM
CONV_WIDTH = 512
CONV_K = 3
D_FF = 4 * D_MODEL
PLE_DIM = 256
EPS = 1e-6

LANES = 128
SUBLANES = 8
VMEM_LIMIT_BYTES = 56 * 1024 * 1024

TOKEN_TILE = 512
KEY_BLOCK = 256
QUERY_BLOCK = 512
FF_CHUNK = 1024
HEAD_PAD = 2 * HEAD_DIM
HEADS_PER_STEP = 2
EXP2_UNDERFLOW = 160.0

BF16 = jnp.bfloat16
F32 = jnp.float32


def _rms_norm(xf, g):
    var = jnp.mean(xf * xf, axis=-1, keepdims=True)
    return xf * lax.rsqrt(var + EPS) * g


def _sigmoid(v):
    return 1.0 / (1.0 + jnp.exp(-v))


def _resident(shape):
    zeros = (0,) * len(shape)
    return pl.BlockSpec(shape, lambda *_: zeros, pipeline_mode=pl.Buffered(1))


def _pre_kernel(x_ref, g_ref, wt_ref, wn_ref, bg_ref, wc_ref, wco_ref,
                qt_ref, k_ref, vt_ref, gate_ref, gconv_ref, ubuf_ref, *, q_scale):
    tm = x_ref.shape[1]
    h = _rms_norm(x_ref[0], g_ref[...]).astype(BF16)

    tr = lax.dot_general(wt_ref[...], h, (((1,), (1,)), ((), ())),
                         preferred_element_type=F32)
    zeros_half = jnp.zeros((HEAD_DIM, tm), BF16)
    for hd in range(N_HEADS):
        q_h = (tr[hd * HEAD_DIM:(hd + 1) * HEAD_DIM] * q_scale).astype(BF16)
        lo = (hd % 2) * HEAD_DIM
        qt_ref[0, hd, lo:lo + HEAD_DIM, :] = q_h
        qt_ref[0, hd, HEAD_DIM - lo:2 * HEAD_DIM - lo, :] = zeros_half
        v_h = tr[ATTN_WIDTH + hd * HEAD_DIM:ATTN_WIDTH + (hd + 1) * HEAD_DIM].astype(BF16)
        for c in range(tm // KEY_BLOCK):
            vt_ref[0, hd, c] = v_h[:, c * KEY_BLOCK:(c + 1) * KEY_BLOCK]

    k_ref[0] = jnp.dot(h, wn_ref[:, 0:ATTN_WIDTH],
                       preferred_element_type=F32).astype(BF16)

    c0 = ATTN_WIDTH
    conv3 = jnp.dot(h, wn_ref[:, c0:c0 + 3 * CONV_WIDTH], preferred_element_type=F32)
    cb = conv3[:, 0:CONV_WIDTH]
    u = conv3[:, CONV_WIDTH:2 * CONV_WIDTH] * conv3[:, 2 * CONV_WIDTH:3 * CONV_WIDTH]

    @pl.when(pl.program_id(1) == 0)
    def _():
        ubuf_ref[0:SUBLANES, :] = jnp.zeros((SUBLANES, CONV_WIDTH), F32)

    ubuf_ref[SUBLANES:SUBLANES + tm, :] = u
    u1 = ubuf_ref[SUBLANES - 1:SUBLANES - 1 + tm, :]
    u2 = ubuf_ref[SUBLANES - 2:SUBLANES - 2 + tm, :]
    ubuf_ref[0:SUBLANES, :] = u[tm - SUBLANES:tm, :]
    wc = wc_ref[...]
    conv = wc[0:1, :] * u2 + wc[1:2, :] * u1 + wc[2:3, :] * u
    y_conv = jnp.dot((cb * conv).astype(BF16), wco_ref[...],
                     preferred_element_type=F32)

    g0 = ATTN_WIDTH + 3 * CONV_WIDTH
    ga = jnp.dot(h, wn_ref[:, g0:g0 + D_MODEL], preferred_element_type=F32)
    gate_ref[0] = _sigmoid(ga + bg_ref[:, 0:D_MODEL])
    gc = jnp.dot(h, wn_ref[:, g0 + D_MODEL:g0 + 2 * D_MODEL], preferred_element_type=F32)
    gconv_ref[0] = _sigmoid(gc + bg_ref[:, D_MODEL:2 * D_MODEL]) * y_conv


def _pre_call(x, g, w_t, w_n, b_gate, w_conv, w_conv_out):
    bsz, seq, _ = x.shape
    tm = TOKEN_TILE
    q_scale = (HEAD_DIM ** -0.5) * math.log2(math.e)
    n_nat = w_n.shape[1]
    out_shape = (
        jax.ShapeDtypeStruct((bsz, N_HEADS, HEAD_PAD, seq), BF16),
        jax.ShapeDtypeStruct((bsz, seq, ATTN_WIDTH), BF16),
        jax.ShapeDtypeStruct((bsz, N_HEADS, seq // KEY_BLOCK, HEAD_DIM, KEY_BLOCK), BF16),
        jax.ShapeDtypeStruct((bsz, seq, D_MODEL), F32),
        jax.ShapeDtypeStruct((bsz, seq, D_MODEL), F32),
    )
    return pl.pallas_call(
        functools.partial(_pre_kernel, q_scale=q_scale),
        out_shape=out_shape,
        grid=(bsz, seq // tm),
        in_specs=[
            pl.BlockSpec((1, tm, D_MODEL), lambda b, i: (b, i, 0)),
            _resident((1, D_MODEL)),
            _resident((2 * ATTN_WIDTH, D_MODEL)),
            _resident((D_MODEL, n_nat)),
            _resident((1, 2 * D_MODEL)),
            _resident((CONV_K, CONV_WIDTH)),
            _resident((CONV_WIDTH, D_MODEL)),
        ],
        out_specs=(
            pl.BlockSpec((1, N_HEADS, HEAD_PAD, tm), lambda b, i: (b, 0, 0, i)),
            pl.BlockSpec((1, tm, ATTN_WIDTH), lambda b, i: (b, i, 0)),
            pl.BlockSpec((1, N_HEADS, tm // KEY_BLOCK, HEAD_DIM, KEY_BLOCK),
                         lambda b, i: (b, 0, i, 0, 0)),
            pl.BlockSpec((1, tm, D_MODEL), lambda b, i: (b, i, 0)),
            pl.BlockSpec((1, tm, D_MODEL), lambda b, i: (b, i, 0)),
        ),
        scratch_shapes=[pltpu.VMEM((tm + SUBLANES, CONV_WIDTH), F32)],
        compiler_params=pltpu.CompilerParams(
            dimension_semantics=("arbitrary", "arbitrary"),
            vmem_limit_bytes=VMEM_LIMIT_BYTES),
        name="pre_proj",
    )(x, g, w_t, w_n, b_gate, w_conv, w_conv_out)


def _softplus2(z):
    return jnp.maximum(z, 0) + jnp.log2(1 + jnp.exp2(-jnp.abs(z)))


def _attn_kernel(qt_ref, k_ref, vt_ref, o_ref,
                 z_ref, d_ref, tot_ref, carry_ref, ahead_ref, acc_ref):
    bq = qt_ref.shape[3]
    bk = KEY_BLOCK
    qi = pl.program_id(2)
    assert bq == 2 * bk
    heads = range(HEADS_PER_STEP)
    top = 2 * qi + 1

    row = lax.broadcasted_iota(jnp.int32, (bk, bk), 0)
    col = lax.broadcasted_iota(jnp.int32, (bk, bk), 1)
    neg_tri = jnp.where(col >= row, -1.0, 0.0).astype(BF16)

    def scores(m, par, diag=None):
        start = pl.multiple_of(jnp.maximum(top - m, 0) * bk, bk)
        for hd in heads:
            pair = (hd // 2) * HEAD_PAD
            kb = k_ref[0, pl.ds(start, bk), pair:pair + HEAD_PAD]
            z = jnp.dot(kb, qt_ref[0, hd], preferred_element_type=F32)
            if diag is not None:
                key_pos = lax.broadcasted_iota(jnp.int32, (bk, bq), 0) + diag * bk
                q_pos = lax.broadcasted_iota(jnp.int32, (bk, bq), 1)
                z = jnp.where(key_pos < q_pos, z, -jnp.inf)
            z_ref[par, hd] = z

    def cumsum(par):
        for hd in heads:
            z = z_ref[par, hd]
            ncsum = jnp.dot(neg_tri, _softplus2(z).astype(BF16), preferred_element_type=F32)
            d_ref[par, hd] = ncsum + z
            tot_ref[par, hd] = ncsum[0:1, :]
            ahead_ref[hd] += ncsum[0:1, :]

    def weights(m, par):
        for hd in heads:
            pv = jnp.dot(vt_ref[0, hd, top - m], jnp.exp2(d_ref[par, hd]).astype(BF16),
                         preferred_element_type=F32)
            ncarry = carry_ref[hd]
            acc_ref[hd] += pv * jnp.exp2(ncarry)
            carry_ref[hd] = ncarry + tot_ref[par, hd]

    carry_ref[...] = jnp.zeros_like(carry_ref)
    ahead_ref[...] = jnp.zeros_like(ahead_ref)
    acc_ref[...] = jnp.zeros_like(acc_ref)
    scores(0, 0, diag=1)
    scores(1, 1, diag=0)
    cumsum(0)

    def step(t, par):
        scores(t + 2, par)
        cumsum(1 - par)
        weights(t, par)
        return jnp.logical_and(t + 2 <= top, jnp.max(ahead_ref[...]) >= -EXP2_UNDERFLOW)

    def body(state):
        t, _ = state
        more = step(t, 0)
        more_after_odd = lax.cond(more, lambda: step(t + 1, 1), lambda: jnp.bool_(False))
        return jnp.where(more, t + 2, t + 1), more_after_odd

    last, _ = lax.while_loop(lambda state: state[1], body, (jnp.int32(0), True))

    @pl.when(jnp.bitwise_and(last, 1) == 0)
    def _():
        weights(last, 0)

    @pl.when(jnp.bitwise_and(last, 1) == 1)
    def _():
        weights(last, 1)
    for hd in heads:
        o_ref[0, hd] = acc_ref[hd].astype(BF16)


def _attn_call(qt, k, vt):
    bsz, _, _, seq = qt.shape
    bq = QUERY_BLOCK
    hp = HEADS_PER_STEP
    return pl.pallas_call(
        _attn_kernel,
        out_shape=jax.ShapeDtypeStruct((bsz, N_HEADS, HEAD_DIM, seq), BF16),
        grid=(bsz, N_HEADS // hp, seq // bq),
        in_specs=[
            pl.BlockSpec((1, hp, HEAD_PAD, bq), lambda b, h, i: (b, h, 0, i)),
            pl.BlockSpec((1, seq, hp * HEAD_DIM), lambda b, h, i: (b, 0, h)),
            pl.BlockSpec((1, hp, seq // KEY_BLOCK, HEAD_DIM, KEY_BLOCK),
                         lambda b, h, i: (b, h, 0, 0, 0)),
        ],
        out_specs=pl.BlockSpec((1, hp, HEAD_DIM, bq), lambda b, h, i: (b, h, 0, i)),
        scratch_shapes=[
            pltpu.VMEM((2, hp, KEY_BLOCK, bq), F32),
            pltpu.VMEM((2, hp, KEY_BLOCK, bq), F32),
            pltpu.VMEM((2, hp, 1, bq), F32),
            pltpu.VMEM((hp, 1, bq), F32),
            pltpu.VMEM((hp, 1, bq), F32),
            pltpu.VMEM((hp, HEAD_DIM, bq), F32),
        ],
        compiler_params=pltpu.CompilerParams(
            dimension_semantics=("arbitrary", "arbitrary", "arbitrary"),
            vmem_limit_bytes=VMEM_LIMIT_BYTES),
        name="stickbreak_attn",
    )(qt, k, vt)


def _post_kernel(x_ref, ot_ref, gate_ref, gconv_ref, p_ref,
                 wao_ref, wo_ref, wup_ref, wdn_ref, wpg_ref, wpp_ref,
                 g1_ref, g2_ref, g3_ref, g4_ref, out_ref):
    tm = x_ref.shape[1]
    y_attn = lax.dot_general(ot_ref[0], wao_ref[...], (((0,), (0,)), ((), ())),
                             preferred_element_type=F32)
    mix_in = gate_ref[0] * y_attn + gconv_ref[0]
    mixed = jnp.dot(mix_in.astype(BF16), wo_ref[...], preferred_element_type=F32)
    x1 = x_ref[0] + _rms_norm(mixed, g1_ref[...])

    h2 = _rms_norm(x1, g2_ref[...]).astype(BF16)
    f = jnp.zeros((tm, D_MODEL), F32)
    for c in range(D_FF // FF_CHUNK):
        up = jnp.dot(h2, wup_ref[:, c * FF_CHUNK:(c + 1) * FF_CHUNK],
                     preferred_element_type=F32)
        act = jnp.square(jnp.maximum(up, 0.0)).astype(BF16)
        f = f + jnp.dot(act, wdn_ref[c * FF_CHUNK:(c + 1) * FF_CHUNK, :],
                        preferred_element_type=F32)
    x2 = x1 + _rms_norm(f, g3_ref[...])

    h3 = _rms_norm(x2, g4_ref[...]).astype(BF16)
    ple_gate = _sigmoid(jnp.dot(h3, wpg_ref[...], preferred_element_type=F32))
    ple = jnp.dot(p_ref[0].astype(BF16), wpp_ref[...], preferred_element_type=F32)
    out_ref[0] = x2 + ple_gate * ple


def _post_call(x, ot, gate, gconv, p, w_ao, w_o, w_up, w_dn, w_pg, w_pp, g1, g2, g3, g4):
    bsz, seq, _ = x.shape
    tm = TOKEN_TILE
    tok = lambda width: pl.BlockSpec((1, tm, width), lambda b, i: (b, i, 0))
    return pl.pallas_call(
        _post_kernel,
        out_shape=jax.ShapeDtypeStruct((bsz, seq, D_MODEL), F32),
        grid=(bsz, seq // tm),
        in_specs=[
            tok(D_MODEL),
            pl.BlockSpec((1, ATTN_WIDTH, tm), lambda b, i: (b, 0, i)),
            tok(D_MODEL),
            tok(D_MODEL),
            tok(PLE_DIM),
            _resident((ATTN_WIDTH, D_MODEL)),
            _resident((D_MODEL, D_MODEL)),
            _resident((D_MODEL, D_FF)),
            _resident((D_FF, D_MODEL)),
            _resident((D_MODEL, D_MODEL)),
            _resident((PLE_DIM, D_MODEL)),
            _resident((1, D_MODEL)),
            _resident((1, D_MODEL)),
            _resident((1, D_MODEL)),
            _resident((1, D_MODEL)),
        ],
        out_specs=tok(D_MODEL),
        compiler_params=pltpu.CompilerParams(
            dimension_semantics=("arbitrary", "arbitrary"),
            vmem_limit_bytes=VMEM_LIMIT_BYTES),
        name="post_mix_mlp",
    )(x, ot, gate, gconv, p, w_ao, w_o, w_up, w_dn, w_pg, w_pp, g1, g2, g3, g4)


def kernel(x, p, g_pre_mix, w_in, b_gate, w_conv, w_attn_out, w_conv_out, w_o,
           g_post_mix, g_pre_mlp, w_up, w_down, g_post_mlp, g_ple, w_ple_gate, w_ple_proj):
    bsz, seq, _ = x.shape
    depth = w_in.shape[0]
    row = lambda v: v.reshape(1, -1)
    for i in range(depth):
        w = w_in[i].astype(BF16)
        w_t = jnp.concatenate([w[:, 0:ATTN_WIDTH], w[:, 2 * ATTN_WIDTH:3 * ATTN_WIDTH]], axis=1).T
        w_n = jnp.concatenate([w[:, ATTN_WIDTH:2 * ATTN_WIDTH], w[:, 3 * ATTN_WIDTH:]], axis=1)
        qt, k, vt, gate, gconv = _pre_call(
            x, row(g_pre_mix[i]), w_t, w_n, row(b_gate[i]), w_conv[i],
            w_conv_out[i].astype(BF16))
        ot = _attn_call(qt, k, vt).reshape(bsz, ATTN_WIDTH, seq)
        x = _post_call(
            x, ot, gate, gconv, p[i],
            w_attn_out[i].astype(BF16), w_o[i].astype(BF16), w_up[i].astype(BF16),
            w_down[i].astype(BF16), w_ple_gate[i].astype(BF16), w_ple_proj[i].astype(BF16),
            row(g_post_mix[i]), row(g_pre_mlp[i]), row(g_post_mlp[i]), row(g_ple[i]))
    return x
```

```python
import functools
import math

import jax
import jax.numpy as jnp
from jax import lax
from jax.experimental import pallas as pl
from jax.experimental.pallas import tpu as pltpu

D_MODEL = 1024
N_HEADS = 8
HEAD_DIM = 64
ATTN_WIDTH = N_HEADS * HEAD_DIM
CONV_WIDTH = 512
CONV_K = 3
D_FF = 4 * D_MODEL
PLE_DIM = 256
EPS = 1e-6

LANES = 128
SUBLANES = 8
VMEM_LIMIT_BYTES = 56 * 1024 * 1024

TOKEN_TILE = 512
ROW_GROUPS = 2
KEY_BLOCK = 256
QUERY_BLOCK = 256
FF_CHUNK = 1024
HEAD_PAD = 2 * HEAD_DIM
HEADS_PER_STEP = 4
EXP2_UNDERFLOW = 160.0

BF16 = jnp.bfloat16
F32 = jnp.float32


def _rms_norm(xf, g):
    var = jnp.mean(xf * xf, axis=-1, keepdims=True)
    return xf * lax.rsqrt(var + EPS) * g


def _sigmoid(v):
    return 1.0 / (1.0 + jnp.exp(-v))


def _resident(shape):
    zeros = (0,) * len(shape)
    return pl.BlockSpec(shape, lambda *_: zeros, pipeline_mode=pl.Buffered(1))


def _pre_kernel(x_ref, g_ref, wt_ref, wn_ref, bg_ref, wc_ref, wco_ref,
                qt_ref, k_ref, vt_ref, gate_ref, gconv_ref, ubuf_ref, *, q_scale):
    tm = x_ref.shape[1]
    tg = KEY_BLOCK
    groups = range(tm // tg)
    c0 = ATTN_WIDTH
    g0 = ATTN_WIDTH + 3 * CONV_WIDTH

    def norm(g):
        return _rms_norm(x_ref[0, pl.ds(g * tg, tg), :], g_ref[...]).astype(BF16)

    def qkv(g, h):
        rows = pl.ds(g * tg, tg)
        tr = lax.dot_general(wt_ref[...], h, (((1,), (1,)), ((), ())),
                             preferred_element_type=F32)
        zeros_half = jnp.zeros((HEAD_DIM, tg), BF16)
        for hd in range(N_HEADS):
            q_h = (tr[hd * HEAD_DIM:(hd + 1) * HEAD_DIM] * q_scale).astype(BF16)
            lo = (hd % 2) * HEAD_DIM
            qt_ref[0, hd, lo:lo + HEAD_DIM, rows] = q_h
            qt_ref[0, hd, HEAD_DIM - lo:2 * HEAD_DIM - lo, rows] = zeros_half
            vt_ref[0, hd, g] = tr[ATTN_WIDTH + hd * HEAD_DIM:
                                  ATTN_WIDTH + (hd + 1) * HEAD_DIM].astype(BF16)
        k_ref[0, rows, :] = jnp.dot(h, wn_ref[:, 0:ATTN_WIDTH],
                                    preferred_element_type=F32).astype(BF16)

    def conv_inputs(h):
        return jnp.dot(h, wn_ref[:, c0:c0 + 3 * CONV_WIDTH], preferred_element_type=F32)

    def conv_branch(g, conv3):
        r0 = SUBLANES + g * tg
        cb = conv3[:, 0:CONV_WIDTH]
        u = conv3[:, CONV_WIDTH:2 * CONV_WIDTH] * conv3[:, 2 * CONV_WIDTH:3 * CONV_WIDTH]
        ubuf_ref[r0:r0 + tg, :] = u
        u1 = ubuf_ref[r0 - 1:r0 - 1 + tg, :]
        u2 = ubuf_ref[r0 - 2:r0 - 2 + tg, :]
        wc = wc_ref[...]
        conv = wc[0:1, :] * u2 + wc[1:2, :] * u1 + wc[2:3, :] * u
        return jnp.dot((cb * conv).astype(BF16), wco_ref[...],
                       preferred_element_type=F32)

    def gates(g, h, y_conv):
        rows = pl.ds(g * tg, tg)
        ga = jnp.dot(h, wn_ref[:, g0:g0 + D_MODEL], preferred_element_type=F32)
        gate_ref[0, rows, :] = _sigmoid(ga + bg_ref[:, 0:D_MODEL]).astype(BF16)
        gc = jnp.dot(h, wn_ref[:, g0 + D_MODEL:g0 + 2 * D_MODEL], preferred_element_type=F32)
        gconv_ref[0, rows, :] = (_sigmoid(gc + bg_ref[:, D_MODEL:2 * D_MODEL])
                                 * y_conv).astype(BF16)

    @pl.when(pl.program_id(1) == 0)
    def _():
        ubuf_ref[0:SUBLANES, :] = jnp.zeros((SUBLANES, CONV_WIDTH), F32)

    hs = [norm(g) for g in groups]
    for g in groups:
        qkv(g, hs[g])
    conv3 = [conv_inputs(hs[g]) for g in groups]
    y_conv = [conv_branch(g, conv3[g]) for g in groups]
    ubuf_ref[0:SUBLANES, :] = ubuf_ref[tm:tm + SUBLANES, :]
    for g in groups:
        gates(g, hs[g], y_conv[g])


def _pre_call(x, g, w_t, w_n, b_gate, w_conv, w_conv_out):
    bsz, seq, _ = x.shape
    tm = TOKEN_TILE
    q_scale = (HEAD_DIM ** -0.5) * math.log2(math.e)
    n_nat = w_n.shape[1]
    out_shape = (
        jax.ShapeDtypeStruct((bsz, N_HEADS, HEAD_PAD, seq), BF16),
        jax.ShapeDtypeStruct((bsz, seq, ATTN_WIDTH), BF16),
        jax.ShapeDtypeStruct((bsz, N_HEADS, seq // KEY_BLOCK, HEAD_DIM, KEY_BLOCK), BF16),
        jax.ShapeDtypeStruct((bsz, seq, D_MODEL), BF16),
        jax.ShapeDtypeStruct((bsz, seq, D_MODEL), BF16),
    )
    return pl.pallas_call(
        functools.partial(_pre_kernel, q_scale=q_scale),
        out_shape=out_shape,
        grid=(bsz, seq // tm),
        in_specs=[
            pl.BlockSpec((1, tm, D_MODEL), lambda b, i: (b, i, 0)),
            _resident((1, D_MODEL)),
            _resident((2 * ATTN_WIDTH, D_MODEL)),
            _resident((D_MODEL, n_nat)),
            _resident((1, 2 * D_MODEL)),
            _resident((CONV_K, CONV_WIDTH)),
            _resident((CONV_WIDTH, D_MODEL)),
        ],
        out_specs=(
            pl.BlockSpec((1, N_HEADS, HEAD_PAD, tm), lambda b, i: (b, 0, 0, i)),
            pl.BlockSpec((1, tm, ATTN_WIDTH), lambda b, i: (b, i, 0)),
            pl.BlockSpec((1, N_HEADS, tm // KEY_BLOCK, HEAD_DIM, KEY_BLOCK),
                         lambda b, i: (b, 0, i, 0, 0)),
            pl.BlockSpec((1, tm, D_MODEL), lambda b, i: (b, i, 0)),
            pl.BlockSpec((1, tm, D_MODEL), lambda b, i: (b, i, 0)),
        ),
        scratch_shapes=[pltpu.VMEM((tm + SUBLANES, CONV_WIDTH), F32)],
        compiler_params=pltpu.CompilerParams(
            dimension_semantics=("arbitrary", "arbitrary"),
            vmem_limit_bytes=VMEM_LIMIT_BYTES),
        name="pre_proj",
    )(x, g, w_t, w_n, b_gate, w_conv, w_conv_out)


def _softplus2(z):
    return jnp.maximum(z, 0) + jnp.log2(1 + jnp.exp2(-jnp.abs(z)))


def _attn_kernel(qt_ref, k_ref, vt_ref, o_ref,
                 z_ref, d_ref, tot_ref, carry_ref, ahead_ref, acc_ref):
    bq = qt_ref.shape[3]
    bk = KEY_BLOCK
    qi = pl.program_id(2)
    assert bq == bk
    heads = range(HEADS_PER_STEP)
    top = qi

    row = lax.broadcasted_iota(jnp.int32, (bk, bk), 0)
    col = lax.broadcasted_iota(jnp.int32, (bk, bk), 1)
    neg_tri = jnp.where(col >= row, -1.0, 0.0).astype(BF16)

    def block_index(m):
        return jnp.maximum(top - m, 0)

    def scores(m, par, diag=False):
        start = pl.multiple_of(block_index(m) * bk, bk)
        for hd in heads:
            pair = (hd // 2) * HEAD_PAD
            kb = k_ref[0, pl.ds(start, bk), pair:pair + HEAD_PAD]
            z = jnp.dot(kb, qt_ref[0, hd], preferred_element_type=F32)
            if diag:
                z = jnp.where(row < col, z, -jnp.inf)
            z_ref[par, hd] = z

    def cumsum(par):
        for hd in heads:
            z = z_ref[par, hd]
            ncsum = jnp.dot(neg_tri, _softplus2(z).astype(BF16), preferred_element_type=F32)
            d_ref[par, hd] = ncsum + z
            tot_ref[par, hd] = ncsum[0:1, :]
            ahead_ref[hd] += ncsum[0:1, :]

    def weights(m, par):
        exists = m <= top
        for hd in heads:
            pv = jnp.dot(vt_ref[0, hd, block_index(m)], jnp.exp2(d_ref[par, hd]).astype(BF16),
                         preferred_element_type=F32)
            ncarry = carry_ref[hd]
            acc_ref[hd] += pv * jnp.where(exists, jnp.exp2(ncarry), 0.0)
            carry_ref[hd] = ncarry + tot_ref[par, hd]

    def more_after(m):
        return jnp.logical_and(m + 1 <= top, jnp.max(ahead_ref[...]) >= -EXP2_UNDERFLOW)

    carry_ref[...] = jnp.zeros_like(carry_ref)
    ahead_ref[...] = jnp.zeros_like(ahead_ref)
    acc_ref[...] = jnp.zeros_like(acc_ref)
    scores(0, 0, diag=True)
    scores(1, 1)
    cumsum(0)
    cumsum(1)
    weights(0, 0)
    weights(1, 1)

    @pl.when(more_after(1))
    def _():
        scores(2, 0)
        scores(3, 1)
        cumsum(0)

        def step(t, par):
            scores(t + 2, par)
            cumsum(1 - par)
            weights(t, par)
            return more_after(t + 1)

        def body(state):
            t, _ = state
            more = step(t, 0)
            more_after_odd = lax.cond(more, lambda: step(t + 1, 1), lambda: jnp.bool_(False))
            return jnp.where(more, t + 2, t + 1), more_after_odd

        last, _ = lax.while_loop(lambda state: state[1], body, (jnp.int32(2), True))

        @pl.when(jnp.bitwise_and(last, 1) == 0)
        def _():
            weights(last, 0)

        @pl.when(jnp.bitwise_and(last, 1) == 1)
        def _():
            weights(last, 1)

    for hd in heads:
        o_ref[0, hd] = acc_ref[hd].astype(BF16)


def _attn_call(qt, k, vt):
    bsz, _, _, seq = qt.shape
    bq = QUERY_BLOCK
    hp = HEADS_PER_STEP
    return pl.pallas_call(
        _attn_kernel,
        out_shape=jax.ShapeDtypeStruct((bsz, N_HEADS, HEAD_DIM, seq), BF16),
        grid=(bsz, N_HEADS // hp, seq // bq),
        in_specs=[
            pl.BlockSpec((1, hp, HEAD_PAD, bq), lambda b, h, i: (b, h, 0, i)),
            pl.BlockSpec((1, seq, hp * HEAD_DIM), lambda b, h, i: (b, 0, h)),
            pl.BlockSpec((1, hp, seq // KEY_BLOCK, HEAD_DIM, KEY_BLOCK),
                         lambda b, h, i: (b, h, 0, 0, 0)),
        ],
        out_specs=pl.BlockSpec((1, hp, HEAD_DIM, bq), lambda b, h, i: (b, h, 0, i)),
        scratch_shapes=[
            pltpu.VMEM((2, hp, KEY_BLOCK, bq), F32),
            pltpu.VMEM((2, hp, KEY_BLOCK, bq), F32),
            pltpu.VMEM((2, hp, 1, bq), F32),
            pltpu.VMEM((hp, 1, bq), F32),
            pltpu.VMEM((hp, 1, bq), F32),
            pltpu.VMEM((hp, HEAD_DIM, bq), F32),
        ],
        compiler_params=pltpu.CompilerParams(
            dimension_semantics=("arbitrary", "arbitrary", "arbitrary"),
            vmem_limit_bytes=VMEM_LIMIT_BYTES),
        name="stickbreak_attn",
    )(qt, k, vt)


def _post_kernel(x_ref, ot_ref, gate_ref, gconv_ref, p_ref,
                 wao_ref, wo_ref, wup_ref, wdn_ref, wpg_ref, wpp_ref,
                 g1_ref, g2_ref, g3_ref, g4_ref, out_ref):
    tm = x_ref.shape[1]
    groups = [pl.ds(r * (tm // ROW_GROUPS), tm // ROW_GROUPS) for r in range(ROW_GROUPS)]

    def each(fn, *lists):
        return [fn(*args) for args in zip(*lists)]

    y_attn = each(lambda rows: lax.dot_general(
        ot_ref[0, :, rows], wao_ref[...], (((0,), (0,)), ((), ())),
        preferred_element_type=F32), groups)
    mixed = each(lambda rows, y: jnp.dot(
        (gate_ref[0, rows, :] * y + gconv_ref[0, rows, :]).astype(BF16), wo_ref[...],
        preferred_element_type=F32), groups, y_attn)
    x1 = each(lambda rows, m: x_ref[0, rows, :] + _rms_norm(m, g1_ref[...]), groups, mixed)

    h2 = each(lambda v: _rms_norm(v, g2_ref[...]).astype(BF16), x1)
    f = [jnp.zeros(v.shape, F32) for v in x1]
    for c in range(D_FF // FF_CHUNK):
        cols = slice(c * FF_CHUNK, (c + 1) * FF_CHUNK)
        up = each(lambda h: jnp.dot(h, wup_ref[:, cols], preferred_element_type=F32), h2)
        f = each(lambda acc, u: acc + jnp.dot(
            jnp.square(jnp.maximum(u, 0.0)).astype(BF16), wdn_ref[cols, :],
            preferred_element_type=F32), f, up)
    x2 = each(lambda v, fv: v + _rms_norm(fv, g3_ref[...]), x1, f)

    proj = each(lambda rows: jnp.dot(p_ref[0, rows, :].astype(BF16), wpp_ref[...],
                                     preferred_element_type=F32), groups)
    gate = each(lambda v: _sigmoid(jnp.dot(_rms_norm(v, g4_ref[...]).astype(BF16), wpg_ref[...],
                                           preferred_element_type=F32)), x2)
    for rows, v, gt, pj in zip(groups, x2, gate, proj):
        out_ref[0, rows, :] = v + gt * pj


def _post_call(x, ot, gate, gconv, p, w_ao, w_o, w_up, w_dn, w_pg, w_pp, g1, g2, g3, g4):
    bsz, seq, _ = x.shape
    tm = TOKEN_TILE
    tok = lambda width: pl.BlockSpec((1, tm, width), lambda b, i: (b, i, 0))
    return pl.pallas_call(
        _post_kernel,
        out_shape=jax.ShapeDtypeStruct((bsz, seq, D_MODEL), F32),
        grid=(bsz, seq // tm),
        in_specs=[
            tok(D_MODEL),
            pl.BlockSpec((1, ATTN_WIDTH, tm), lambda b, i: (b, 0, i)),
            tok(D_MODEL),
            tok(D_MODEL),
            tok(PLE_DIM),
            _resident((ATTN_WIDTH, D_MODEL)),
            _resident((D_MODEL, D_MODEL)),
            _resident((D_MODEL, D_FF)),
            _resident((D_FF, D_MODEL)),
            _resident((D_MODEL, D_MODEL)),
            _resident((PLE_DIM, D_MODEL)),
            _resident((1, D_MODEL)),
            _resident((1, D_MODEL)),
            _resident((1, D_MODEL)),
            _resident((1, D_MODEL)),
        ],
        out_specs=tok(D_MODEL),
        compiler_params=pltpu.CompilerParams(
            dimension_semantics=("arbitrary", "arbitrary"),
            vmem_limit_bytes=VMEM_LIMIT_BYTES),
        name="post_mix_mlp",
    )(x, ot, gate, gconv, p, w_ao, w_o, w_up, w_dn, w_pg, w_pp, g1, g2, g3, g4)


def kernel(x, p, g_pre_mix, w_in, b_gate, w_conv, w_attn_out, w_conv_out, w_o,
           g_post_mix, g_pre_mlp, w_up, w_down, g_post_mlp, g_ple, w_ple_gate, w_ple_proj):
    bsz, seq, _ = x.shape
    depth = w_in.shape[0]
    row = lambda v: v.reshape(1, -1)
    for i in range(depth):
        w = w_in[i].astype(BF16)
        w_t = jnp.concatenate([w[:, 0:ATTN_WIDTH], w[:, 2 * ATTN_WIDTH:3 * ATTN_WIDTH]], axis=1).T
        w_n = jnp.concatenate([w[:, ATTN_WIDTH:2 * ATTN_WIDTH], w[:, 3 * ATTN_WIDTH:]], axis=1)
        qt, k, vt, gate, gconv = _pre_call(
            x, row(g_pre_mix[i]), w_t, w_n, row(b_gate[i]), w_conv[i],
            w_conv_out[i].astype(BF16))
        ot = _attn_call(qt, k, vt).reshape(bsz, ATTN_WIDTH, seq)
        x = _post_call(
            x, ot, gate, gconv, p[i],
            w_attn_out[i].astype(BF16), w_o[i].astype(BF16), w_up[i].astype(BF16),
            w_down[i].astype(BF16), w_ple_gate[i].astype(BF16), w_ple_proj[i].astype(BF16),
            row(g_post_mix[i]), row(g_pre_mlp[i]), row(g_post_mlp[i]), row(g_ple[i]))
    return x
```

```python
import functools
import math

import jax
import jax.numpy as jnp
from jax import lax
from jax.experimental import pallas as pl
from jax.experimental.pallas import tpu as pltpu

D_MODEL = 1024
N_HEADS = 8
HEAD_DIM = 64
ATTN_WIDTH = N_HEADS * HEAD_DIM
CONV_WIDTH = 512
CONV_K = 3
D_FF = 4 * D_MODEL
PLE_DIM = 256
EPS = 1e-6

LANES = 128
SUBLANES = 8
VMEM_LIMIT_BYTES = 56 * 1024 * 1024

TOKEN_TILE = 512
ROW_GROUPS = 2
KEY_BLOCK = 256
FF_CHUNK = 1024
HEAD_PAD = 2 * HEAD_DIM
ATTN_HEAD_GROUP = 4
EXP2_UNDERFLOW = 160.0

BF16 = jnp.bfloat16
F32 = jnp.float32


def _rms_norm(xf, g):
    var = jnp.mean(xf * xf, axis=-1, keepdims=True)
    return xf * lax.rsqrt(var + EPS) * g


def _sigmoid(v):
    return 1.0 / (1.0 + jnp.exp(-v))


def _resident(shape):
    zeros = (0,) * len(shape)
    return pl.BlockSpec(shape, lambda *_: zeros, pipeline_mode=pl.Buffered(1))


def _softplus2(z):
    return jnp.maximum(z, 0) + jnp.log2(1 + jnp.exp2(-jnp.abs(z)))


def _front_kernel(x_ref, g_ref, wt_ref, wn_ref, bg_ref, wc_ref, wco_ref,
                  gate_ref, gconv_ref, o_ref,
                  ubuf_ref, kall_ref, vall_ref, qprev_ref, qnext_ref,
                  z_ref, d_ref, tot_ref, carry_ref, ahead_ref, acc_ref, *, q_scale):
    tm = x_ref.shape[1]
    bk = KEY_BLOCK
    tg = bk
    groups = range(tm // tg)
    i = pl.program_id(1)
    n_tiles = pl.num_programs(1) - 1
    tile = jnp.minimum(i, n_tiles - 1)
    c0 = ATTN_WIDTH
    g0 = ATTN_WIDTH + 3 * CONV_WIDTH

    row = lax.broadcasted_iota(jnp.int32, (bk, bk), 0)
    col = lax.broadcasted_iota(jnp.int32, (bk, bk), 1)
    neg_tri = jnp.where(col >= row, -1.0, 0.0).astype(BF16)

    @pl.when(i == 0)
    def _():
        ubuf_ref[0:SUBLANES, :] = jnp.zeros((SUBLANES, CONV_WIDTH), F32)
        qprev_ref[...] = jnp.zeros_like(qprev_ref)
        kall_ref[0:bk, :] = jnp.zeros((bk, ATTN_WIDTH), BF16)
        vall_ref[:, 0] = jnp.zeros((N_HEADS, HEAD_DIM, bk), BF16)

    def norm(g):
        return _rms_norm(x_ref[0, pl.ds(g * tg, tg), :], g_ref[...]).astype(BF16)

    def qkv(g, h):
        cols = pl.ds(g * tg, tg)
        tr = lax.dot_general(wt_ref[...], h, (((1,), (1,)), ((), ())),
                             preferred_element_type=F32)
        zeros_half = jnp.zeros((HEAD_DIM, tg), BF16)
        for hd in range(N_HEADS):
            q_h = (tr[hd * HEAD_DIM:(hd + 1) * HEAD_DIM] * q_scale).astype(BF16)
            lo = (hd % 2) * HEAD_DIM
            qnext_ref[hd, lo:lo + HEAD_DIM, cols] = q_h
            qnext_ref[hd, HEAD_DIM - lo:2 * HEAD_DIM - lo, cols] = zeros_half
            vall_ref[hd, (tm // bk) * tile + g] = tr[ATTN_WIDTH + hd * HEAD_DIM:
                                                     ATTN_WIDTH + (hd + 1) * HEAD_DIM].astype(BF16)
        start = pl.multiple_of(tile * tm + g * tg, tg)
        kall_ref[pl.ds(start, tg), :] = jnp.dot(h, wn_ref[:, 0:ATTN_WIDTH],
                                                preferred_element_type=F32).astype(BF16)

    def conv_inputs(h):
        return jnp.dot(h, wn_ref[:, c0:c0 + 3 * CONV_WIDTH], preferred_element_type=F32)

    def conv_branch(g, conv3):
        r0 = SUBLANES + g * tg
        cb = conv3[:, 0:CONV_WIDTH]
        u = conv3[:, CONV_WIDTH:2 * CONV_WIDTH] * conv3[:, 2 * CONV_WIDTH:3 * CONV_WIDTH]
        ubuf_ref[r0:r0 + tg, :] = u
        u1 = ubuf_ref[r0 - 1:r0 - 1 + tg, :]
        u2 = ubuf_ref[r0 - 2:r0 - 2 + tg, :]
        wc = wc_ref[...]
        conv = wc[0:1, :] * u2 + wc[1:2, :] * u1 + wc[2:3, :] * u
        return jnp.dot((cb * conv).astype(BF16), wco_ref[...],
                       preferred_element_type=F32)

    def gate_attn(g, h):
        ga = jnp.dot(h, wn_ref[:, g0:g0 + D_MODEL], preferred_element_type=F32)
        gate_ref[0, pl.ds(g * tg, tg), :] = _sigmoid(ga + bg_ref[:, 0:D_MODEL]).astype(BF16)

    def gate_conv(g, h, y_conv):
        gc = jnp.dot(h, wn_ref[:, g0 + D_MODEL:g0 + 2 * D_MODEL], preferred_element_type=F32)
        gconv_ref[0, pl.ds(g * tg, tg), :] = (_sigmoid(gc + bg_ref[:, D_MODEL:2 * D_MODEL])
                                              * y_conv).astype(BF16)

    calls = [(a, hg) for a in range(tm // bk) for hg in range(N_HEADS // ATTN_HEAD_GROUP)]
    local = range(ATTN_HEAD_GROUP)

    def call_top(a):
        return (tm // bk) * (i - 1) + a

    def key_block(top, m, hd):
        start = pl.multiple_of(jnp.maximum(top - m, 0) * bk, bk)
        pair = (hd // 2) * HEAD_PAD
        return kall_ref[pl.ds(start, bk), pair:pair + HEAD_PAD]

    def value_block(top, m, hd):
        return vall_ref[hd, jnp.maximum(top - m, 0)]

    def query_tile(a, hd):
        return qprev_ref[hd, :, a * bk:(a + 1) * bk]

    def cumsum_of(z):
        ncsum = jnp.dot(neg_tri, _softplus2(z).astype(BF16), preferred_element_type=F32)
        return ncsum + z, ncsum[0:1, :]

    z_vals, d_vals = {}, {}

    def scores01(c):
        a, hg = calls[c]
        top = call_top(a)
        out = []
        for hl in local:
            hd = hg * ATTN_HEAD_GROUP + hl
            q = query_tile(a, hd)
            z0 = jnp.dot(key_block(top, 0, hd), q, preferred_element_type=F32)
            z0 = jnp.where(row < col, z0, -jnp.inf)
            z1 = jnp.dot(key_block(top, 1, hd), q, preferred_element_type=F32)
            out.append((z0, z1))
        z_vals[c] = out

    def cumsum01(c):
        d_vals[c] = [cumsum_of(z0) + cumsum_of(z1) for z0, z1 in z_vals[c]]

    def weights01(c):
        a, hg = calls[c]
        top = call_top(a)
        for hl in local:
            hd = hg * ATTN_HEAD_GROUP + hl
            d0, t0, d1, t1 = d_vals[c][hl]
            pv0 = jnp.dot(value_block(top, 0, hd), jnp.exp2(d0).astype(BF16),
                          preferred_element_type=F32)
            pv1 = jnp.dot(value_block(top, 1, hd), jnp.exp2(d1).astype(BF16),
                          preferred_element_type=F32)
            acc_ref[c, hl] = (jnp.where(top >= 0, pv0, 0.0)
                              + jnp.where(top >= 1, pv1 * jnp.exp2(t0), 0.0))
            carry_ref[c, hl] = t0 + t1
            ahead_ref[c, hl] = t0 + t1

    def further_blocks(c):
        a, hg = calls[c]
        top = call_top(a)

        def more_after(m):
            return jnp.logical_and(m + 1 <= top, jnp.max(ahead_ref[c]) >= -EXP2_UNDERFLOW)

        def scores(m, par):
            for hl in local:
                hd = hg * ATTN_HEAD_GROUP + hl
                z_ref[par, hl] = jnp.dot(key_block(top, m, hd), query_tile(a, hd),
                                         preferred_element_type=F32)

        def cumsum(par):
            for hl in local:
                d, t = cumsum_of(z_ref[par, hl])
                d_ref[par, hl] = d
                tot_ref[par, hl] = t
                ahead_ref[c, hl] += t

        def weights(m, par):
            exists = m <= top
            for hl in local:
                hd = hg * ATTN_HEAD_GROUP + hl
                pv = jnp.dot(value_block(top, m, hd), jnp.exp2(d_ref[par, hl]).astype(BF16),
                             preferred_element_type=F32)
                ncarry = carry_ref[c, hl]
                acc_ref[c, hl] += jnp.where(exists, pv * jnp.exp2(ncarry), 0.0)
                carry_ref[c, hl] = ncarry + tot_ref[par, hl]

        @pl.when(more_after(1))
        def _():
            scores(2, 0)
            scores(3, 1)
            cumsum(0)

            def step(t, par):
                scores(t + 2, par)
                cumsum(1 - par)
                weights(t, par)
                return more_after(t + 1)

            def body(state):
                t, _ = state
                more = step(t, 0)
                more_after_odd = lax.cond(more, lambda: step(t + 1, 1), lambda: jnp.bool_(False))
                return jnp.where(more, t + 2, t + 1), more_after_odd

            last, _ = lax.while_loop(lambda state: state[1], body, (jnp.int32(2), True))

            @pl.when(jnp.bitwise_and(last, 1) == 0)
            def _():
                weights(last, 0)

            @pl.when(jnp.bitwise_and(last, 1) == 1)
            def _():
                weights(last, 1)

    hs = [norm(g) for g in groups]
    scores01(0)
    qkv(0, hs[0])
    cumsum01(0)
    qkv(1, hs[1])
    weights01(0)
    scores01(1)
    conv3_0 = conv_inputs(hs[0])
    cumsum01(1)
    conv3_1 = conv_inputs(hs[1])
    weights01(1)
    scores01(2)
    y_conv = [conv_branch(0, conv3_0), conv_branch(1, conv3_1)]
    ubuf_ref[0:SUBLANES, :] = ubuf_ref[tm:tm + SUBLANES, :]
    cumsum01(2)
    gate_attn(0, hs[0])
    weights01(2)
    scores01(3)
    gate_attn(1, hs[1])
    cumsum01(3)
    gate_conv(0, hs[0], y_conv[0])
    weights01(3)
    gate_conv(1, hs[1], y_conv[1])

    for c in range(len(calls)):
        further_blocks(c)

    for c, (a, hg) in enumerate(calls):
        for hl in local:
            o_ref[0, hg * ATTN_HEAD_GROUP + hl, :, a * bk:(a + 1) * bk] = acc_ref[c, hl].astype(BF16)
    qprev_ref[...] = qnext_ref[...]


def _front_call(x, g, w_t, w_n, b_gate, w_conv, w_conv_out):
    bsz, seq, _ = x.shape
    tm = TOKEN_TILE
    n_tiles = seq // tm
    n_calls = (tm // KEY_BLOCK) * (N_HEADS // ATTN_HEAD_GROUP)
    q_scale = (HEAD_DIM ** -0.5) * math.log2(math.e)
    out_shape = (
        jax.ShapeDtypeStruct((bsz, seq + tm, D_MODEL), BF16),
        jax.ShapeDtypeStruct((bsz, seq + tm, D_MODEL), BF16),
        jax.ShapeDtypeStruct((bsz, N_HEADS, HEAD_DIM, seq), BF16),
    )
    return pl.pallas_call(
        functools.partial(_front_kernel, q_scale=q_scale),
        out_shape=out_shape,
        grid=(bsz, n_tiles + 1),
        in_specs=[
            pl.BlockSpec((1, tm, D_MODEL), lambda b, i: (b, jnp.minimum(i, n_tiles - 1), 0)),
            _resident((1, D_MODEL)),
            _resident((2 * ATTN_WIDTH, D_MODEL)),
            _resident((D_MODEL, w_n.shape[1])),
            _resident((1, 2 * D_MODEL)),
            _resident((CONV_K, CONV_WIDTH)),
            _resident((CONV_WIDTH, D_MODEL)),
        ],
        out_specs=(
            pl.BlockSpec((1, tm, D_MODEL), lambda b, i: (b, i, 0)),
            pl.BlockSpec((1, tm, D_MODEL), lambda b, i: (b, i, 0)),
            pl.BlockSpec((1, N_HEADS, HEAD_DIM, tm), lambda b, i: (b, 0, 0, jnp.maximum(i - 1, 0))),
        ),
        scratch_shapes=[
            pltpu.VMEM((tm + SUBLANES, CONV_WIDTH), F32),
            pltpu.VMEM((seq, ATTN_WIDTH), BF16),
            pltpu.VMEM((N_HEADS, seq // KEY_BLOCK, HEAD_DIM, KEY_BLOCK), BF16),
            pltpu.VMEM((N_HEADS, HEAD_PAD, tm), BF16),
            pltpu.VMEM((N_HEADS, HEAD_PAD, tm), BF16),
            pltpu.VMEM((2, ATTN_HEAD_GROUP, KEY_BLOCK, KEY_BLOCK), F32),
            pltpu.VMEM((2, ATTN_HEAD_GROUP, KEY_BLOCK, KEY_BLOCK), F32),
            pltpu.VMEM((2, ATTN_HEAD_GROUP, 1, KEY_BLOCK), F32),
            pltpu.VMEM((n_calls, ATTN_HEAD_GROUP, 1, KEY_BLOCK), F32),
            pltpu.VMEM((n_calls, ATTN_HEAD_GROUP, 1, KEY_BLOCK), F32),
            pltpu.VMEM((n_calls, ATTN_HEAD_GROUP, HEAD_DIM, KEY_BLOCK), F32),
        ],
        compiler_params=pltpu.CompilerParams(
            dimension_semantics=("arbitrary", "arbitrary"),
            vmem_limit_bytes=VMEM_LIMIT_BYTES),
        name="front_proj_attn",
    )(x, g, w_t, w_n, b_gate, w_conv, w_conv_out)


def _post_kernel(x_ref, ot_ref, gate_ref, gconv_ref, p_ref,
                 wao_ref, wo_ref, wup_ref, wdn_ref, wpg_ref, wpp_ref,
                 g1_ref, g2_ref, g3_ref, g4_ref, out_ref):
    tm = x_ref.shape[1]
    groups = [pl.ds(r * (tm // ROW_GROUPS), tm // ROW_GROUPS) for r in range(ROW_GROUPS)]

    def each(fn, *lists):
        return [fn(*args) for args in zip(*lists)]

    y_attn = each(lambda rows: lax.dot_general(
        ot_ref[0, :, rows], wao_ref[...], (((0,), (0,)), ((), ())),
        preferred_element_type=F32), groups)
    mixed = each(lambda rows, y: jnp.dot(
        (gate_ref[0, rows, :] * y + gconv_ref[0, rows, :]).astype(BF16), wo_ref[...],
        preferred_element_type=F32), groups, y_attn)
    x1 = each(lambda rows, m: x_ref[0, rows, :] + _rms_norm(m, g1_ref[...]), groups, mixed)

    h2 = each(lambda v: _rms_norm(v, g2_ref[...]).astype(BF16), x1)
    f = [jnp.zeros(v.shape, F32) for v in x1]
    for c in range(D_FF // FF_CHUNK):
        cols = slice(c * FF_CHUNK, (c + 1) * FF_CHUNK)
        up = each(lambda h: jnp.dot(h, wup_ref[:, cols], preferred_element_type=F32), h2)
        f = each(lambda acc, u: acc + jnp.dot(
            jnp.square(jnp.maximum(u, 0.0)).astype(BF16), wdn_ref[cols, :],
            preferred_element_type=F32), f, up)
    x2 = each(lambda v, fv: v + _rms_norm(fv, g3_ref[...]), x1, f)

    proj = each(lambda rows: jnp.dot(p_ref[0, rows, :].astype(BF16), wpp_ref[...],
                                     preferred_element_type=F32), groups)
    gate = each(lambda v: _sigmoid(jnp.dot(_rms_norm(v, g4_ref[...]).astype(BF16), wpg_ref[...],
                                           preferred_element_type=F32)), x2)
    for rows, v, gt, pj in zip(groups, x2, gate, proj):
        out_ref[0, rows, :] = v + gt * pj


def _post_call(x, ot, gate, gconv, p, w_ao, w_o, w_up, w_dn, w_pg, w_pp, g1, g2, g3, g4):
    bsz, seq, _ = x.shape
    tm = TOKEN_TILE
    tok = lambda width: pl.BlockSpec((1, tm, width), lambda b, i: (b, i, 0))
    return pl.pallas_call(
        _post_kernel,
        out_shape=jax.ShapeDtypeStruct((bsz, seq, D_MODEL), F32),
        grid=(bsz, seq // tm),
        in_specs=[
            tok(D_MODEL),
            pl.BlockSpec((1, ATTN_WIDTH, tm), lambda b, i: (b, 0, i)),
            tok(D_MODEL),
            tok(D_MODEL),
            tok(PLE_DIM),
            _resident((ATTN_WIDTH, D_MODEL)),
            _resident((D_MODEL, D_MODEL)),
            _resident((D_MODEL, D_FF)),
            _resident((D_FF, D_MODEL)),
            _resident((D_MODEL, D_MODEL)),
            _resident((PLE_DIM, D_MODEL)),
            _resident((1, D_MODEL)),
            _resident((1, D_MODEL)),
            _resident((1, D_MODEL)),
            _resident((1, D_MODEL)),
        ],
        out_specs=tok(D_MODEL),
        compiler_params=pltpu.CompilerParams(
            dimension_semantics=("arbitrary", "arbitrary"),
            vmem_limit_bytes=VMEM_LIMIT_BYTES),
        name="post_mix_mlp",
    )(x, ot, gate, gconv, p, w_ao, w_o, w_up, w_dn, w_pg, w_pp, g1, g2, g3, g4)


def kernel(x, p, g_pre_mix, w_in, b_gate, w_conv, w_attn_out, w_conv_out, w_o,
           g_post_mix, g_pre_mlp, w_up, w_down, g_post_mlp, g_ple, w_ple_gate, w_ple_proj):
    bsz, seq, _ = x.shape
    depth = w_in.shape[0]
    row = lambda v: v.reshape(1, -1)
    for i in range(depth):
        w = w_in[i].astype(BF16)
        w_t = jnp.concatenate([w[:, 0:ATTN_WIDTH], w[:, 2 * ATTN_WIDTH:3 * ATTN_WIDTH]], axis=1).T
        w_n = jnp.concatenate([w[:, ATTN_WIDTH:2 * ATTN_WIDTH], w[:, 3 * ATTN_WIDTH:]], axis=1)
        gate, gconv, ot = _front_call(
            x, row(g_pre_mix[i]), w_t, w_n, row(b_gate[i]), w_conv[i],
            w_conv_out[i].astype(BF16))
        ot = ot.reshape(bsz, ATTN_WIDTH, seq)
        x = _post_call(
            x, ot, gate, gconv, p[i],
            w_attn_out[i].astype(BF16), w_o[i].astype(BF16), w_up[i].astype(BF16),
            w_down[i].astype(BF16), w_ple_gate[i].astype(BF16), w_ple_proj[i].astype(BF16),
            row(g_post_mix[i]), row(g_pre_mlp[i]), row(g_post_mlp[i]), row(g_ple[i]))
    return x
```

```python
import functools
import math

import jax
import jax.numpy as jnp
from jax import lax
from jax.experimental import pallas as pl
from jax.experimental.pallas import tpu as pltpu

D_MODEL = 1024
N_HEADS = 8
HEAD_DIM = 64
ATTN_WIDTH = N_HEADS * HEAD_DIM
CONV_WIDTH = 512
CONV_K = 3
D_FF = 4 * D_MODEL
PLE_DIM = 256
EPS = 1e-6

LANES = 128
SUBLANES = 8
VMEM_LIMIT_BYTES = 56 * 1024 * 1024

TOKEN_TILE = 512
ROW_GROUPS = 2
KEY_BLOCK = 256
FF_CHUNK = 1024
HEAD_PAD = 2 * HEAD_DIM
ATTN_HEAD_GROUP = 4
EXP2_UNDERFLOW = 160.0

BF16 = jnp.bfloat16
F32 = jnp.float32


def _rms_norm(xf, g):
    var = jnp.mean(xf * xf, axis=-1, keepdims=True)
    return xf * lax.rsqrt(var + EPS) * g


def _sigmoid(v):
    return 1.0 / (1.0 + jnp.exp(-v))


def _resident(shape):
    zeros = (0,) * len(shape)
    return pl.BlockSpec(shape, lambda *_: zeros, pipeline_mode=pl.Buffered(1))


def _softplus2(z):
    return jnp.maximum(z, 0) + jnp.log2(1 + jnp.exp2(-jnp.abs(z)))


def _front_kernel(x_ref, g_ref, wt_ref, wn_ref, bg_ref, wc_ref, wco_ref,
                  gate_ref, gconv_ref, o_ref,
                  ubuf_ref, kall_ref, vall_ref, qprev_ref, qnext_ref,
                  z_ref, d_ref, tot_ref, carry_ref, ahead_ref, acc_ref, *, q_scale):
    tm = x_ref.shape[1]
    bk = KEY_BLOCK
    tg = bk
    groups = range(tm // tg)
    i = pl.program_id(1)
    n_tiles = pl.num_programs(1) - 1
    tile = jnp.minimum(i, n_tiles - 1)
    c0 = ATTN_WIDTH
    g0 = ATTN_WIDTH + 3 * CONV_WIDTH

    row = lax.broadcasted_iota(jnp.int32, (bk, bk), 0)
    col = lax.broadcasted_iota(jnp.int32, (bk, bk), 1)
    neg_tri = jnp.where(col >= row, -1.0, 0.0).astype(BF16)

    @pl.when(i == 0)
    def _():
        ubuf_ref[0:SUBLANES, :] = jnp.zeros((SUBLANES, CONV_WIDTH), F32)
        qprev_ref[...] = jnp.zeros_like(qprev_ref)
        kall_ref[0:bk, :] = jnp.zeros((bk, ATTN_WIDTH), BF16)
        vall_ref[:, 0] = jnp.zeros((N_HEADS, HEAD_DIM, bk), BF16)

    def norm(g):
        return _rms_norm(x_ref[0, pl.ds(g * tg, tg), :], g_ref[...]).astype(BF16)

    def qkv(g, h):
        cols = pl.ds(g * tg, tg)
        tr = lax.dot_general(wt_ref[...], h, (((1,), (1,)), ((), ())),
                             preferred_element_type=F32)
        zeros_half = jnp.zeros((HEAD_DIM, tg), BF16)
        for hd in range(N_HEADS):
            q_h = (tr[hd * HEAD_DIM:(hd + 1) * HEAD_DIM] * q_scale).astype(BF16)
            lo = (hd % 2) * HEAD_DIM
            qnext_ref[hd, lo:lo + HEAD_DIM, cols] = q_h
            qnext_ref[hd, HEAD_DIM - lo:2 * HEAD_DIM - lo, cols] = zeros_half
            vall_ref[hd, (tm // bk) * tile + g] = tr[ATTN_WIDTH + hd * HEAD_DIM:
                                                     ATTN_WIDTH + (hd + 1) * HEAD_DIM].astype(BF16)
        start = pl.multiple_of(tile * tm + g * tg, tg)
        kall_ref[pl.ds(start, tg), :] = jnp.dot(h, wn_ref[:, 0:ATTN_WIDTH],
                                                preferred_element_type=F32).astype(BF16)

    def conv_inputs(h):
        return jnp.dot(h, wn_ref[:, c0:c0 + 3 * CONV_WIDTH], preferred_element_type=F32)

    def conv_branch(g, conv3):
        r0 = SUBLANES + g * tg
        cb = conv3[:, 0:CONV_WIDTH]
        u = conv3[:, CONV_WIDTH:2 * CONV_WIDTH] * conv3[:, 2 * CONV_WIDTH:3 * CONV_WIDTH]
        ubuf_ref[r0:r0 + tg, :] = u
        u1 = ubuf_ref[r0 - 1:r0 - 1 + tg, :]
        u2 = ubuf_ref[r0 - 2:r0 - 2 + tg, :]
        wc = wc_ref[...]
        conv = wc[0:1, :] * u2 + wc[1:2, :] * u1 + wc[2:3, :] * u
        return jnp.dot((cb * conv).astype(BF16), wco_ref[...],
                       preferred_element_type=F32)

    def gate_attn(g, h):
        ga = jnp.dot(h, wn_ref[:, g0:g0 + D_MODEL], preferred_element_type=F32)
        gate_ref[0, pl.ds(g * tg, tg), :] = _sigmoid(ga + bg_ref[:, 0:D_MODEL]).astype(BF16)

    def gate_conv(g, h, y_conv):
        gc = jnp.dot(h, wn_ref[:, g0 + D_MODEL:g0 + 2 * D_MODEL], preferred_element_type=F32)
        gconv_ref[0, pl.ds(g * tg, tg), :] = (_sigmoid(gc + bg_ref[:, D_MODEL:2 * D_MODEL])
                                              * y_conv).astype(BF16)

    calls = [(a, hg) for a in range(tm // bk) for hg in range(N_HEADS // ATTN_HEAD_GROUP)]
    local = range(ATTN_HEAD_GROUP)

    def call_top(a):
        return (tm // bk) * (i - 1) + a

    def key_block(top, m, hd):
        start = pl.multiple_of(jnp.maximum(top - m, 0) * bk, bk)
        pair = (hd // 2) * HEAD_PAD
        return kall_ref[pl.ds(start, bk), pair:pair + HEAD_PAD]

    def value_block(top, m, hd):
        return vall_ref[hd, jnp.maximum(top - m, 0)]

    def query_tile(a, hd):
        return qprev_ref[hd, :, a * bk:(a + 1) * bk]

    def cumsum_of(z):
        ncsum = jnp.dot(neg_tri, _softplus2(z).astype(BF16), preferred_element_type=F32)
        return ncsum + z, ncsum[0:1, :]

    z_vals, d_vals = {}, {}

    def scores01(c):
        a, hg = calls[c]
        top = call_top(a)
        out = []
        for hl in local:
            hd = hg * ATTN_HEAD_GROUP + hl
            q = query_tile(a, hd)
            z0 = jnp.dot(key_block(top, 0, hd), q, preferred_element_type=F32)
            z0 = jnp.where(row < col, z0, -jnp.inf)
            z1 = jnp.dot(key_block(top, 1, hd), q, preferred_element_type=F32)
            out.append((z0, z1))
        z_vals[c] = out

    def cumsum01(c):
        d_vals[c] = [cumsum_of(z0) + cumsum_of(z1) for z0, z1 in z_vals[c]]

    def weights01(c):
        a, hg = calls[c]
        top = call_top(a)
        for hl in local:
            hd = hg * ATTN_HEAD_GROUP + hl
            d0, t0, d1, t1 = d_vals[c][hl]
            pv0 = jnp.dot(value_block(top, 0, hd), jnp.exp2(d0).astype(BF16),
                          preferred_element_type=F32)
            pv1 = jnp.dot(value_block(top, 1, hd), jnp.exp2(d1).astype(BF16),
                          preferred_element_type=F32)
            acc_ref[c, hl] = (jnp.where(top >= 0, pv0, 0.0)
                              + jnp.where(top >= 1, pv1 * jnp.exp2(t0), 0.0))
            carry_ref[c, hl] = t0 + t1
            ahead_ref[c, hl] = t0 + t1

    def further_blocks(c):
        a, hg = calls[c]
        top = call_top(a)

        def more_after(m):
            return jnp.logical_and(m + 1 <= top, jnp.max(ahead_ref[c]) >= -EXP2_UNDERFLOW)

        def scores(m, par):
            for hl in local:
                hd = hg * ATTN_HEAD_GROUP + hl
                z_ref[par, hl] = jnp.dot(key_block(top, m, hd), query_tile(a, hd),
                                         preferred_element_type=F32)

        def cumsum(par):
            for hl in local:
                d, t = cumsum_of(z_ref[par, hl])
                d_ref[par, hl] = d
                tot_ref[par, hl] = t
                ahead_ref[c, hl] += t

        def weights(m, par):
            exists = m <= top
            for hl in local:
                hd = hg * ATTN_HEAD_GROUP + hl
                pv = jnp.dot(value_block(top, m, hd), jnp.exp2(d_ref[par, hl]).astype(BF16),
                             preferred_element_type=F32)
                ncarry = carry_ref[c, hl]
                acc_ref[c, hl] += jnp.where(exists, pv * jnp.exp2(ncarry), 0.0)
                carry_ref[c, hl] = ncarry + tot_ref[par, hl]

        @pl.when(more_after(1))
        def _():
            scores(2, 0)
            scores(3, 1)
            cumsum(0)

            def step(t, par):
                scores(t + 2, par)
                cumsum(1 - par)
                weights(t, par)
                return more_after(t + 1)

            def body(state):
                t, _ = state
                more = step(t, 0)
                more_after_odd = lax.cond(more, lambda: step(t + 1, 1), lambda: jnp.bool_(False))
                return jnp.where(more, t + 2, t + 1), more_after_odd

            last, _ = lax.while_loop(lambda state: state[1], body, (jnp.int32(2), True))

            @pl.when(jnp.bitwise_and(last, 1) == 0)
            def _():
                weights(last, 0)

            @pl.when(jnp.bitwise_and(last, 1) == 1)
            def _():
                weights(last, 1)

    hs = [norm(g) for g in groups]
    scores01(0)
    qkv(0, hs[0])
    cumsum01(0)
    qkv(1, hs[1])
    weights01(0)
    scores01(1)
    conv3_0 = conv_inputs(hs[0])
    cumsum01(1)
    conv3_1 = conv_inputs(hs[1])
    weights01(1)
    scores01(2)
    y_conv = [conv_branch(0, conv3_0), conv_branch(1, conv3_1)]
    ubuf_ref[0:SUBLANES, :] = ubuf_ref[tm:tm + SUBLANES, :]
    cumsum01(2)
    gate_attn(0, hs[0])
    weights01(2)
    scores01(3)
    gate_attn(1, hs[1])
    cumsum01(3)
    gate_conv(0, hs[0], y_conv[0])
    weights01(3)
    gate_conv(1, hs[1], y_conv[1])

    last_top = call_top(tm // bk - 1)
    @pl.when(jnp.logical_and(2 <= last_top, jnp.max(ahead_ref[...]) >= -EXP2_UNDERFLOW))
    def _():
        for c in range(len(calls)):
            further_blocks(c)

    for c, (a, hg) in enumerate(calls):
        for hl in local:
            o_ref[0, hg * ATTN_HEAD_GROUP + hl, :, a * bk:(a + 1) * bk] = acc_ref[c, hl].astype(BF16)
    qprev_ref[...] = qnext_ref[...]


def _front_call(x, g, w_t, w_n, b_gate, w_conv, w_conv_out):
    bsz, seq, _ = x.shape
    tm = TOKEN_TILE
    n_tiles = seq // tm
    n_calls = (tm // KEY_BLOCK) * (N_HEADS // ATTN_HEAD_GROUP)
    q_scale = (HEAD_DIM ** -0.5) * math.log2(math.e)
    out_shape = (
        jax.ShapeDtypeStruct((bsz, seq + tm, D_MODEL), BF16),
        jax.ShapeDtypeStruct((bsz, seq + tm, D_MODEL), BF16),
        jax.ShapeDtypeStruct((bsz, N_HEADS, HEAD_DIM, seq), BF16),
    )
    return pl.pallas_call(
        functools.partial(_front_kernel, q_scale=q_scale),
        out_shape=out_shape,
        grid=(bsz, n_tiles + 1),
        in_specs=[
            pl.BlockSpec((1, tm, D_MODEL), lambda b, i: (b, jnp.minimum(i, n_tiles - 1), 0)),
            _resident((1, D_MODEL)),
            _resident((2 * ATTN_WIDTH, D_MODEL)),
            _resident((D_MODEL, w_n.shape[1])),
            _resident((1, 2 * D_MODEL)),
            _resident((CONV_K, CONV_WIDTH)),
            _resident((CONV_WIDTH, D_MODEL)),
        ],
        out_specs=(
            pl.BlockSpec((1, tm, D_MODEL), lambda b, i: (b, i, 0)),
            pl.BlockSpec((1, tm, D_MODEL), lambda b, i: (b, i, 0)),
            pl.BlockSpec((1, N_HEADS, HEAD_DIM, tm), lambda b, i: (b, 0, 0, jnp.maximum(i - 1, 0))),
        ),
        scratch_shapes=[
            pltpu.VMEM((tm + SUBLANES, CONV_WIDTH), F32),
            pltpu.VMEM((seq, ATTN_WIDTH), BF16),
            pltpu.VMEM((N_HEADS, seq // KEY_BLOCK, HEAD_DIM, KEY_BLOCK), BF16),
            pltpu.VMEM((N_HEADS, HEAD_PAD, tm), BF16),
            pltpu.VMEM((N_HEADS, HEAD_PAD, tm), BF16),
            pltpu.VMEM((2, ATTN_HEAD_GROUP, KEY_BLOCK, KEY_BLOCK), F32),
            pltpu.VMEM((2, ATTN_HEAD_GROUP, KEY_BLOCK, KEY_BLOCK), F32),
            pltpu.VMEM((2, ATTN_HEAD_GROUP, 1, KEY_BLOCK), F32),
            pltpu.VMEM((n_calls, ATTN_HEAD_GROUP, 1, KEY_BLOCK), F32),
            pltpu.VMEM((n_calls, ATTN_HEAD_GROUP, 1, KEY_BLOCK), F32),
            pltpu.VMEM((n_calls, ATTN_HEAD_GROUP, HEAD_DIM, KEY_BLOCK), F32),
        ],
        compiler_params=pltpu.CompilerParams(
            dimension_semantics=("arbitrary", "arbitrary"),
            vmem_limit_bytes=VMEM_LIMIT_BYTES),
        name="front_proj_attn",
    )(x, g, w_t, w_n, b_gate, w_conv, w_conv_out)


def _post_kernel(x_ref, ot_ref, gate_ref, gconv_ref, p_ref,
                 wao_ref, wo_ref, wup_ref, wdn_ref, wpg_ref, wpp_ref,
                 g1_ref, g2_ref, g3_ref, g4_ref, out_ref):
    tm = x_ref.shape[1]
    groups = [pl.ds(r * (tm // ROW_GROUPS), tm // ROW_GROUPS) for r in range(ROW_GROUPS)]

    def each(fn, *lists):
        return [fn(*args) for args in zip(*lists)]

    y_attn = each(lambda rows: lax.dot_general(
        ot_ref[0, :, rows], wao_ref[...], (((0,), (0,)), ((), ())),
        preferred_element_type=F32), groups)
    mixed = each(lambda rows, y: jnp.dot(
        (gate_ref[0, rows, :] * y + gconv_ref[0, rows, :]).astype(BF16), wo_ref[...],
        preferred_element_type=F32), groups, y_attn)
    x1 = each(lambda rows, m: x_ref[0, rows, :] + _rms_norm(m, g1_ref[...]), groups, mixed)

    h2 = each(lambda v: _rms_norm(v, g2_ref[...]).astype(BF16), x1)
    f = [jnp.zeros(v.shape, F32) for v in x1]
    for c in range(D_FF // FF_CHUNK):
        cols = slice(c * FF_CHUNK, (c + 1) * FF_CHUNK)
        up = each(lambda h: jnp.dot(h, wup_ref[:, cols], preferred_element_type=F32), h2)
        f = each(lambda acc, u: acc + jnp.dot(
            jnp.square(jnp.maximum(u, 0.0)).astype(BF16), wdn_ref[cols, :],
            preferred_element_type=F32), f, up)
    x2 = each(lambda v, fv: v + _rms_norm(fv, g3_ref[...]), x1, f)

    proj = each(lambda rows: jnp.dot(p_ref[0, rows, :].astype(BF16), wpp_ref[...],
                                     preferred_element_type=F32), groups)
    gate = each(lambda v: _sigmoid(jnp.dot(_rms_norm(v, g4_ref[...]).astype(BF16), wpg_ref[...],
                                           preferred_element_type=F32)), x2)
    for rows, v, gt, pj in zip(groups, x2, gate, proj):
        out_ref[0, rows, :] = v + gt * pj


def _post_call(x, ot, gate, gconv, p, w_ao, w_o, w_up, w_dn, w_pg, w_pp, g1, g2, g3, g4):
    bsz, seq, _ = x.shape
    tm = TOKEN_TILE
    tok = lambda width: pl.BlockSpec((1, tm, width), lambda b, i: (b, i, 0))
    return pl.pallas_call(
        _post_kernel,
        out_shape=jax.ShapeDtypeStruct((bsz, seq, D_MODEL), F32),
        grid=(bsz, seq // tm),
        in_specs=[
            tok(D_MODEL),
            pl.BlockSpec((1, ATTN_WIDTH, tm), lambda b, i: (b, 0, i)),
            tok(D_MODEL),
            tok(D_MODEL),
            tok(PLE_DIM),
            _resident((ATTN_WIDTH, D_MODEL)),
            _resident((D_MODEL, D_MODEL)),
            _resident((D_MODEL, D_FF)),
            _resident((D_FF, D_MODEL)),
            _resident((D_MODEL, D_MODEL)),
            _resident((PLE_DIM, D_MODEL)),
            _resident((1, D_MODEL)),
            _resident((1, D_MODEL)),
            _resident((1, D_MODEL)),
            _resident((1, D_MODEL)),
        ],
        out_specs=tok(D_MODEL),
        compiler_params=pltpu.CompilerParams(
            dimension_semantics=("arbitrary", "arbitrary"),
            vmem_limit_bytes=VMEM_LIMIT_BYTES),
        name="post_mix_mlp",
    )(x, ot, gate, gconv, p, w_ao, w_o, w_up, w_dn, w_pg, w_pp, g1, g2, g3, g4)


def kernel(x, p, g_pre_mix, w_in, b_gate, w_conv, w_attn_out, w_conv_out, w_o,
           g_post_mix, g_pre_mlp, w_up, w_down, g_post_mlp, g_ple, w_ple_gate, w_ple_proj):
    bsz, seq, _ = x.shape
    depth = w_in.shape[0]
    row = lambda v: v.reshape(1, -1)
    for i in range(depth):
        w = w_in[i].astype(BF16)
        w_t = jnp.concatenate([w[:, 0:ATTN_WIDTH], w[:, 2 * ATTN_WIDTH:3 * ATTN_WIDTH]], axis=1).T
        w_n = jnp.concatenate([w[:, ATTN_WIDTH:2 * ATTN_WIDTH], w[:, 3 * ATTN_WIDTH:]], axis=1)
        gate, gconv, ot = _front_call(
            x, row(g_pre_mix[i]), w_t, w_n, row(b_gate[i]), w_conv[i],
            w_conv_out[i].astype(BF16))
        ot = ot.reshape(bsz, ATTN_WIDTH, seq)
        x = _post_call(
            x, ot, gate, gconv, p[i],
            w_attn_out[i].astype(BF16), w_o[i].astype(BF16), w_up[i].astype(BF16),
            w_down[i].astype(BF16), w_ple_gate[i].astype(BF16), w_ple_proj[i].astype(BF16),
            row(g_post_mix[i]), row(g_pre_mlp[i]), row(g_post_mlp[i]), row(g_ple[i]))
    return x
```

```python
import functools
import math

import jax
import jax.numpy as jnp
from jax import lax
from jax.experimental import pallas as pl
from jax.experimental.pallas import tpu as pltpu

D_MODEL = 1024
N_HEADS = 8
HEAD_DIM = 64
ATTN_WIDTH = N_HEADS * HEAD_DIM
CONV_WIDTH = 512
CONV_K = 3
D_FF = 4 * D_MODEL
PLE_DIM = 256
EPS = 1e-6

LANES = 128
SUBLANES = 8
VMEM_LIMIT_BYTES = 56 * 1024 * 1024

TOKEN_TILE = 512
ROW_GROUPS = 2
KEY_BLOCK = 256
HEAD_PAD = 2 * HEAD_DIM
ATTN_HEAD_GROUP = 4
EXP2_UNDERFLOW = 160.0

BF16 = jnp.bfloat16
F32 = jnp.float32


def _rms_norm(xf, g):
    var = jnp.mean(xf * xf, axis=-1, keepdims=True)
    return xf * lax.rsqrt(var + EPS) * g


def _sigmoid(v):
    return 1.0 / (1.0 + jnp.exp(-v))


def _resident(shape):
    zeros = (0,) * len(shape)
    return pl.BlockSpec(shape, lambda *_: zeros, pipeline_mode=pl.Buffered(1))


def _softplus2(z):
    return jnp.maximum(z, 0) + jnp.log2(1 + jnp.exp2(-jnp.abs(z)))


def _front_kernel(x_ref, g_ref, wt_ref, wn_ref, bg_ref, wc_ref, wco_ref,
                  gate_ref, gconv_ref, o_ref,
                  ubuf_ref, kall_ref, vall_ref, q_ref,
                  z_ref, d_ref, tot_ref, carry_ref, ahead_ref, acc_ref, *, q_scale):
    tm = x_ref.shape[1]
    bk = KEY_BLOCK
    tg = bk
    groups = range(tm // tg)
    i = pl.program_id(1)
    n_tiles = pl.num_programs(1) - 1
    tile = jnp.minimum(i, n_tiles - 1)
    q_write = jnp.bitwise_and(i, 1)
    c0 = ATTN_WIDTH
    g0 = ATTN_WIDTH + 3 * CONV_WIDTH

    row = lax.broadcasted_iota(jnp.int32, (bk, bk), 0)
    col = lax.broadcasted_iota(jnp.int32, (bk, bk), 1)
    neg_tri = jnp.where(col >= row, -1.0, 0.0).astype(BF16)

    @pl.when(i == 0)
    def _():
        ubuf_ref[0:SUBLANES, :] = jnp.zeros((SUBLANES, CONV_WIDTH), F32)
        q_ref[1] = jnp.zeros(q_ref.shape[1:], BF16)
        kall_ref[0:bk, :] = jnp.zeros((bk, ATTN_WIDTH), BF16)
        vall_ref[:, 0] = jnp.zeros((N_HEADS, HEAD_DIM, bk), BF16)

    def norm(g):
        return _rms_norm(x_ref[0, pl.ds(g * tg, tg), :], g_ref[...]).astype(BF16)

    def qkv(g, h):
        cols = pl.ds(g * tg, tg)
        tr = lax.dot_general(wt_ref[...], h, (((1,), (1,)), ((), ())),
                             preferred_element_type=F32)
        zeros_half = jnp.zeros((HEAD_DIM, tg), BF16)
        for hd in range(N_HEADS):
            q_h = (tr[hd * HEAD_DIM:(hd + 1) * HEAD_DIM] * q_scale).astype(BF16)
            lo = (hd % 2) * HEAD_DIM
            q_ref[q_write, hd, lo:lo + HEAD_DIM, cols] = q_h
            q_ref[q_write, hd, HEAD_DIM - lo:2 * HEAD_DIM - lo, cols] = zeros_half
            vall_ref[hd, (tm // bk) * tile + g] = tr[ATTN_WIDTH + hd * HEAD_DIM:
                                                     ATTN_WIDTH + (hd + 1) * HEAD_DIM].astype(BF16)
        start = pl.multiple_of(tile * tm + g * tg, tg)
        kall_ref[pl.ds(start, tg), :] = jnp.dot(h, wn_ref[:, 0:ATTN_WIDTH],
                                                preferred_element_type=F32).astype(BF16)

    def conv_inputs(h):
        return jnp.dot(h, wn_ref[:, c0:c0 + 3 * CONV_WIDTH], preferred_element_type=F32)

    def conv_branch(g, conv3):
        r0 = SUBLANES + g * tg
        cb = conv3[:, 0:CONV_WIDTH]
        u = conv3[:, CONV_WIDTH:2 * CONV_WIDTH] * conv3[:, 2 * CONV_WIDTH:3 * CONV_WIDTH]
        ubuf_ref[r0:r0 + tg, :] = u
        u1 = ubuf_ref[r0 - 1:r0 - 1 + tg, :]
        u2 = ubuf_ref[r0 - 2:r0 - 2 + tg, :]
        wc = wc_ref[...]
        conv = wc[0:1, :] * u2 + wc[1:2, :] * u1 + wc[2:3, :] * u
        return jnp.dot((cb * conv).astype(BF16), wco_ref[...],
                       preferred_element_type=F32)

    def gate_attn(g, h):
        ga = jnp.dot(h, wn_ref[:, g0:g0 + D_MODEL], preferred_element_type=F32)
        gate_ref[0, pl.ds(g * tg, tg), :] = _sigmoid(ga + bg_ref[:, 0:D_MODEL]).astype(BF16)

    def gate_conv(g, h, y_conv):
        gc = jnp.dot(h, wn_ref[:, g0 + D_MODEL:g0 + 2 * D_MODEL], preferred_element_type=F32)
        gconv_ref[0, pl.ds(g * tg, tg), :] = (_sigmoid(gc + bg_ref[:, D_MODEL:2 * D_MODEL])
                                              * y_conv).astype(BF16)

    calls = [(a, hg) for a in range(tm // bk) for hg in range(N_HEADS // ATTN_HEAD_GROUP)]
    local = range(ATTN_HEAD_GROUP)

    def call_top(a):
        return (tm // bk) * (i - 1) + a

    def key_block(top, m, hd):
        start = pl.multiple_of(jnp.maximum(top - m, 0) * bk, bk)
        pair = (hd // 2) * HEAD_PAD
        return kall_ref[pl.ds(start, bk), pair:pair + HEAD_PAD]

    def value_block(top, m, hd):
        return vall_ref[hd, jnp.maximum(top - m, 0)]

    def query_tile(a, hd):
        return q_ref[1 - q_write, hd, :, a * bk:(a + 1) * bk]

    def cumsum_of(z):
        ncsum = jnp.dot(neg_tri, _softplus2(z).astype(BF16), preferred_element_type=F32)
        return ncsum + z, ncsum[0:1, :]

    z_vals, d_vals = {}, {}

    def scores01(c):
        a, hg = calls[c]
        top = call_top(a)
        out = []
        for hl in local:
            hd = hg * ATTN_HEAD_GROUP + hl
            q = query_tile(a, hd)
            z0 = jnp.dot(key_block(top, 0, hd), q, preferred_element_type=F32)
            z0 = jnp.where(row < col, z0, -jnp.inf)
            z1 = jnp.dot(key_block(top, 1, hd), q, preferred_element_type=F32)
            out.append((z0, z1))
        z_vals[c] = out

    def cumsum01(c):
        d_vals[c] = [cumsum_of(z0) + cumsum_of(z1) for z0, z1 in z_vals[c]]

    def weights01(c):
        a, hg = calls[c]
        top = call_top(a)
        for hl in local:
            hd = hg * ATTN_HEAD_GROUP + hl
            d0, t0, d1, t1 = d_vals[c][hl]
            pv0 = jnp.dot(value_block(top, 0, hd), jnp.exp2(d0).astype(BF16),
                          preferred_element_type=F32)
            pv1 = jnp.dot(value_block(top, 1, hd), jnp.exp2(d1).astype(BF16),
                          preferred_element_type=F32)
            acc_ref[c, hl] = (jnp.where(top >= 0, pv0, 0.0)
                              + jnp.where(top >= 1, pv1 * jnp.exp2(t0), 0.0))
            carry_ref[c, hl] = t0 + t1
            ahead_ref[c, hl] = t0 + t1

    def further_blocks(c):
        a, hg = calls[c]
        top = call_top(a)

        def more_after(m):
            return jnp.logical_and(m + 1 <= top, jnp.max(ahead_ref[c]) >= -EXP2_UNDERFLOW)

        def scores(m, par):
            for hl in local:
                hd = hg * ATTN_HEAD_GROUP + hl
                z_ref[par, hl] = jnp.dot(key_block(top, m, hd), query_tile(a, hd),
                                         preferred_element_type=F32)

        def cumsum(par):
            for hl in local:
                d, t = cumsum_of(z_ref[par, hl])
                d_ref[par, hl] = d
                tot_ref[par, hl] = t
                ahead_ref[c, hl] += t

        def weights(m, par):
            exists = m <= top
            for hl in local:
                hd = hg * ATTN_HEAD_GROUP + hl
                pv = jnp.dot(value_block(top, m, hd), jnp.exp2(d_ref[par, hl]).astype(BF16),
                             preferred_element_type=F32)
                ncarry = carry_ref[c, hl]
                acc_ref[c, hl] += jnp.where(exists, pv * jnp.exp2(ncarry), 0.0)
                carry_ref[c, hl] = ncarry + tot_ref[par, hl]

        @pl.when(more_after(1))
        def _():
            scores(2, 0)
            scores(3, 1)
            cumsum(0)

            def step(t, par):
                scores(t + 2, par)
                cumsum(1 - par)
                weights(t, par)
                return more_after(t + 1)

            def body(state):
                t, _ = state
                more = step(t, 0)
                more_after_odd = lax.cond(more, lambda: step(t + 1, 1), lambda: jnp.bool_(False))
                return jnp.where(more, t + 2, t + 1), more_after_odd

            last, _ = lax.while_loop(lambda state: state[1], body, (jnp.int32(2), True))

            @pl.when(jnp.bitwise_and(last, 1) == 0)
            def _():
                weights(last, 0)

            @pl.when(jnp.bitwise_and(last, 1) == 1)
            def _():
                weights(last, 1)

    hs = [norm(g) for g in groups]
    scores01(0)
    qkv(0, hs[0])
    cumsum01(0)
    qkv(1, hs[1])
    weights01(0)
    scores01(1)
    conv3_0 = conv_inputs(hs[0])
    cumsum01(1)
    conv3_1 = conv_inputs(hs[1])
    weights01(1)
    scores01(2)
    y_conv = [conv_branch(0, conv3_0), conv_branch(1, conv3_1)]
    ubuf_ref[0:SUBLANES, :] = ubuf_ref[tm:tm + SUBLANES, :]
    cumsum01(2)
    gate_attn(0, hs[0])
    weights01(2)
    scores01(3)
    gate_attn(1, hs[1])
    cumsum01(3)
    gate_conv(0, hs[0], y_conv[0])
    weights01(3)
    gate_conv(1, hs[1], y_conv[1])

    last_top = call_top(tm // bk - 1)
    @pl.when(jnp.logical_and(2 <= last_top, jnp.max(ahead_ref[...]) >= -EXP2_UNDERFLOW))
    def _():
        for c in range(len(calls)):
            further_blocks(c)

    for c, (a, hg) in enumerate(calls):
        for hl in local:
            o_ref[0, hg * ATTN_HEAD_GROUP + hl, :, a * bk:(a + 1) * bk] = acc_ref[c, hl].astype(BF16)


def _front_call(x, g, w_t, w_n, b_gate, w_conv, w_conv_out):
    bsz, seq, _ = x.shape
    tm = TOKEN_TILE
    n_tiles = seq // tm
    n_calls = (tm // KEY_BLOCK) * (N_HEADS // ATTN_HEAD_GROUP)
    q_scale = (HEAD_DIM ** -0.5) * math.log2(math.e)
    out_shape = (
        jax.ShapeDtypeStruct((bsz, seq + tm, D_MODEL), BF16),
        jax.ShapeDtypeStruct((bsz, seq + tm, D_MODEL), BF16),
        jax.ShapeDtypeStruct((bsz, N_HEADS, HEAD_DIM, seq), BF16),
    )
    return pl.pallas_call(
        functools.partial(_front_kernel, q_scale=q_scale),
        out_shape=out_shape,
        grid=(bsz, n_tiles + 1),
        in_specs=[
            pl.BlockSpec((1, tm, D_MODEL), lambda b, i: (b, jnp.minimum(i, n_tiles - 1), 0)),
            _resident((1, D_MODEL)),
            _resident((2 * ATTN_WIDTH, D_MODEL)),
            _resident((D_MODEL, w_n.shape[1])),
            _resident((1, 2 * D_MODEL)),
            _resident((CONV_K, CONV_WIDTH)),
            _resident((CONV_WIDTH, D_MODEL)),
        ],
        out_specs=(
            pl.BlockSpec((1, tm, D_MODEL), lambda b, i: (b, i, 0)),
            pl.BlockSpec((1, tm, D_MODEL), lambda b, i: (b, i, 0)),
            pl.BlockSpec((1, N_HEADS, HEAD_DIM, tm), lambda b, i: (b, 0, 0, jnp.maximum(i - 1, 0))),
        ),
        scratch_shapes=[
            pltpu.VMEM((tm + SUBLANES, CONV_WIDTH), F32),
            pltpu.VMEM((seq, ATTN_WIDTH), BF16),
            pltpu.VMEM((N_HEADS, seq // KEY_BLOCK, HEAD_DIM, KEY_BLOCK), BF16),
            pltpu.VMEM((2, N_HEADS, HEAD_PAD, tm), BF16),
            pltpu.VMEM((2, ATTN_HEAD_GROUP, KEY_BLOCK, KEY_BLOCK), F32),
            pltpu.VMEM((2, ATTN_HEAD_GROUP, KEY_BLOCK, KEY_BLOCK), F32),
            pltpu.VMEM((2, ATTN_HEAD_GROUP, 1, KEY_BLOCK), F32),
            pltpu.VMEM((n_calls, ATTN_HEAD_GROUP, 1, KEY_BLOCK), F32),
            pltpu.VMEM((n_calls, ATTN_HEAD_GROUP, 1, KEY_BLOCK), F32),
            pltpu.VMEM((n_calls, ATTN_HEAD_GROUP, HEAD_DIM, KEY_BLOCK), F32),
        ],
        compiler_params=pltpu.CompilerParams(
            dimension_semantics=("arbitrary", "arbitrary"),
            vmem_limit_bytes=VMEM_LIMIT_BYTES),
        name="front_proj_attn",
    )(x, g, w_t, w_n, b_gate, w_conv, w_conv_out)


def _post_kernel(x_ref, ot_ref, gate_ref, gconv_ref, p_ref,
                 wao_ref, wo_ref, wup_ref, wdn_ref, wpg_ref, wpp_ref,
                 g1_ref, g2_ref, g3_ref, g4_ref, out_ref):
    tm = x_ref.shape[1]
    groups = [pl.ds(r * (tm // ROW_GROUPS), tm // ROW_GROUPS) for r in range(ROW_GROUPS)]

    def each(fn, *lists):
        return [fn(*args) for args in zip(*lists)]

    y_attn = each(lambda rows: lax.dot_general(
        ot_ref[0, :, rows], wao_ref[...], (((0,), (0,)), ((), ())),
        preferred_element_type=F32), groups)
    mixed = each(lambda rows, y: jnp.dot(
        (gate_ref[0, rows, :] * y + gconv_ref[0, rows, :]).astype(BF16), wo_ref[...],
        preferred_element_type=F32), groups, y_attn)
    x1 = each(lambda rows, m: x_ref[0, rows, :] + _rms_norm(m, g1_ref[...]), groups, mixed)

    h2 = each(lambda v: _rms_norm(v, g2_ref[...]).astype(BF16), x1)
    up = each(lambda h: jnp.dot(h, wup_ref[...], preferred_element_type=F32), h2)
    f = each(lambda u: jnp.dot(jnp.square(jnp.maximum(u, 0.0)).astype(BF16), wdn_ref[...],
                               preferred_element_type=F32), up)
    x2 = each(lambda v, fv: v + _rms_norm(fv, g3_ref[...]), x1, f)

    gate = each(lambda v: _sigmoid(jnp.dot(_rms_norm(v, g4_ref[...]).astype(BF16), wpg_ref[...],
                                           preferred_element_type=F32)), x2)
    proj = each(lambda rows: jnp.dot(p_ref[0, rows, :].astype(BF16), wpp_ref[...],
                                     preferred_element_type=F32), groups)
    for rows, v, gt, pj in zip(groups, x2, gate, proj):
        out_ref[0, rows, :] = v + gt * pj


def _post_call(x, ot, gate, gconv, p, w_ao, w_o, w_up, w_dn, w_pg, w_pp, g1, g2, g3, g4):
    bsz, seq, _ = x.shape
    tm = TOKEN_TILE
    tok = lambda width: pl.BlockSpec((1, tm, width), lambda b, i: (b, i, 0))
    return pl.pallas_call(
        _post_kernel,
        out_shape=jax.ShapeDtypeStruct((bsz, seq, D_MODEL), F32),
        grid=(bsz, seq // tm),
        in_specs=[
            tok(D_MODEL),
            pl.BlockSpec((1, ATTN_WIDTH, tm), lambda b, i: (b, 0, i)),
            tok(D_MODEL),
            tok(D_MODEL),
            tok(PLE_DIM),
            _resident((ATTN_WIDTH, D_MODEL)),
            _resident((D_MODEL, D_MODEL)),
            _resident((D_MODEL, D_FF)),
            _resident((D_FF, D_MODEL)),
            _resident((D_MODEL, D_MODEL)),
            _resident((PLE_DIM, D_MODEL)),
            _resident((1, D_MODEL)),
            _resident((1, D_MODEL)),
            _resident((1, D_MODEL)),
            _resident((1, D_MODEL)),
        ],
        out_specs=tok(D_MODEL),
        compiler_params=pltpu.CompilerParams(
            dimension_semantics=("arbitrary", "arbitrary"),
            vmem_limit_bytes=VMEM_LIMIT_BYTES),
        name="post_mix_mlp",
    )(x, ot, gate, gconv, p, w_ao, w_o, w_up, w_dn, w_pg, w_pp, g1, g2, g3, g4)


def kernel(x, p, g_pre_mix, w_in, b_gate, w_conv, w_attn_out, w_conv_out, w_o,
           g_post_mix, g_pre_mlp, w_up, w_down, g_post_mlp, g_ple, w_ple_gate, w_ple_proj):
    bsz, seq, _ = x.shape
    depth = w_in.shape[0]
    row = lambda v: v.reshape(1, -1)
    for i in range(depth):
        w = w_in[i].astype(BF16)
        w_t = jnp.concatenate([w[:, 0:ATTN_WIDTH], w[:, 2 * ATTN_WIDTH:3 * ATTN_WIDTH]], axis=1).T
        w_n = jnp.concatenate([w[:, ATTN_WIDTH:2 * ATTN_WIDTH], w[:, 3 * ATTN_WIDTH:]], axis=1)
        gate, gconv, ot = _front_call(
            x, row(g_pre_mix[i]), w_t, w_n, row(b_gate[i]), w_conv[i],
            w_conv_out[i].astype(BF16))
        ot = ot.reshape(bsz, ATTN_WIDTH, seq)
        x = _post_call(
            x, ot, gate, gconv, p[i],
            w_attn_out[i].astype(BF16), w_o[i].astype(BF16), w_up[i].astype(BF16),
            w_down[i].astype(BF16), w_ple_gate[i].astype(BF16), w_ple_proj[i].astype(BF16),
            row(g_post_mix[i]), row(g_pre_mlp[i]), row(g_post_mlp[i]), row(g_ple[i]))
    return x
```

```python
import functools
import math

import jax
import jax.numpy as jnp
from jax import lax
from jax.experimental import pallas as pl
from jax.experimental.pallas import tpu as pltpu

D_MODEL = 1024
N_HEADS = 8
HEAD_DIM = 64
ATTN_WIDTH = N_HEADS * HEAD_DIM
CONV_WIDTH = 512
CONV_K = 3
D_FF = 4 * D_MODEL
PLE_DIM = 256
EPS = 1e-6

SUBLANES = 8
VMEM_LIMIT_BYTES = 56 * 1024 * 1024

TOKEN_TILE = 512
ROW_GROUPS = 2
KEY_BLOCK = 256
HEAD_PAD = 2 * HEAD_DIM
ATTN_HEAD_GROUP = 4
EXP2_UNDERFLOW = 160.0

BF16 = jnp.bfloat16
F32 = jnp.float32


def _rms_norm(xf, g):
    var = jnp.mean(xf * xf, axis=-1, keepdims=True)
    return xf * lax.rsqrt(var + EPS) * g


def _sigmoid(v):
    return 1.0 / (1.0 + jnp.exp(-v))


def _resident(shape):
    zeros = (0,) * len(shape)
    return pl.BlockSpec(shape, lambda *_: zeros, pipeline_mode=pl.Buffered(1))


def _softplus2(z):
    return jnp.maximum(z, 0) + jnp.log2(1 + jnp.exp2(-jnp.abs(z)))


def _front_kernel(x_ref, g_ref, wt_ref, wn_ref, bg_ref, wc_ref, wco_ref,
                  gate_ref, gconv_ref, o_ref,
                  ubuf_ref, kall_ref, vall_ref, q_ref,
                  z_ref, d_ref, tot_ref, carry_ref, ahead_ref, acc_ref, *, q_scale):
    tm = x_ref.shape[1]
    bk = KEY_BLOCK
    tg = bk
    groups = range(tm // tg)
    i = pl.program_id(1)
    c0 = ATTN_WIDTH
    g0 = ATTN_WIDTH + 3 * CONV_WIDTH

    row = lax.broadcasted_iota(jnp.int32, (bk, bk), 0)
    col = lax.broadcasted_iota(jnp.int32, (bk, bk), 1)
    neg_tri = jnp.where(col >= row, -1.0, 0.0).astype(BF16)

    @pl.when(i == 0)
    def _():
        ubuf_ref[0:SUBLANES, :] = jnp.zeros((SUBLANES, CONV_WIDTH), F32)

    def norm(g):
        return _rms_norm(x_ref[0, pl.ds(g * tg, tg), :], g_ref[...]).astype(BF16)

    def qkv(g, h):
        cols = pl.ds(g * tg, tg)
        tr = lax.dot_general(wt_ref[...], h, (((1,), (1,)), ((), ())),
                             preferred_element_type=F32)
        zeros_half = jnp.zeros((HEAD_DIM, tg), BF16)
        for hd in range(N_HEADS):
            q_h = (tr[hd * HEAD_DIM:(hd + 1) * HEAD_DIM] * q_scale).astype(BF16)
            lo = (hd % 2) * HEAD_DIM
            q_ref[hd, lo:lo + HEAD_DIM, cols] = q_h
            q_ref[hd, HEAD_DIM - lo:2 * HEAD_DIM - lo, cols] = zeros_half
            vall_ref[hd, (tm // bk) * i + g] = tr[ATTN_WIDTH + hd * HEAD_DIM:
                                                  ATTN_WIDTH + (hd + 1) * HEAD_DIM].astype(BF16)
        start = pl.multiple_of(i * tm + g * tg, tg)
        kall_ref[pl.ds(start, tg), :] = jnp.dot(h, wn_ref[:, 0:ATTN_WIDTH],
                                                preferred_element_type=F32).astype(BF16)

    def conv_inputs(h):
        return jnp.dot(h, wn_ref[:, c0:c0 + 3 * CONV_WIDTH], preferred_element_type=F32)

    def conv_branch(g, conv3):
        r0 = SUBLANES + g * tg
        cb = conv3[:, 0:CONV_WIDTH]
        u = conv3[:, CONV_WIDTH:2 * CONV_WIDTH] * conv3[:, 2 * CONV_WIDTH:3 * CONV_WIDTH]
        ubuf_ref[r0:r0 + tg, :] = u
        u1 = ubuf_ref[r0 - 1:r0 - 1 + tg, :]
        u2 = ubuf_ref[r0 - 2:r0 - 2 + tg, :]
        wc = wc_ref[...]
        conv = wc[0:1, :] * u2 + wc[1:2, :] * u1 + wc[2:3, :] * u
        return jnp.dot((cb * conv).astype(BF16), wco_ref[...],
                       preferred_element_type=F32)

    def gate_attn(g, h):
        ga = jnp.dot(h, wn_ref[:, g0:g0 + D_MODEL], preferred_element_type=F32)
        gate_ref[0, pl.ds(g * tg, tg), :] = _sigmoid(ga + bg_ref[:, 0:D_MODEL]).astype(BF16)

    def gate_conv(g, h, y_conv):
        gc = jnp.dot(h, wn_ref[:, g0 + D_MODEL:g0 + 2 * D_MODEL], preferred_element_type=F32)
        gconv_ref[0, pl.ds(g * tg, tg), :] = (_sigmoid(gc + bg_ref[:, D_MODEL:2 * D_MODEL])
                                              * y_conv).astype(BF16)

    calls = [(a, hg) for a in range(tm // bk) for hg in range(N_HEADS // ATTN_HEAD_GROUP)]
    local = range(ATTN_HEAD_GROUP)

    def call_top(a):
        return (tm // bk) * i + a

    def key_block(top, m, hd):
        start = pl.multiple_of(jnp.maximum(top - m, 0) * bk, bk)
        pair = (hd // 2) * HEAD_PAD
        return kall_ref[pl.ds(start, bk), pair:pair + HEAD_PAD]

    def value_block(top, m, hd):
        return vall_ref[hd, jnp.maximum(top - m, 0)]

    def query_tile(a, hd):
        return q_ref[hd, :, a * bk:(a + 1) * bk]

    def cumsum_of(z):
        ncsum = jnp.dot(neg_tri, _softplus2(z).astype(BF16), preferred_element_type=F32)
        return ncsum + z, ncsum[0:1, :]

    z_vals, d_vals = {}, {}

    def scores01(c):
        a, hg = calls[c]
        top = call_top(a)
        out = []
        for hl in local:
            hd = hg * ATTN_HEAD_GROUP + hl
            q = query_tile(a, hd)
            z0 = jnp.dot(key_block(top, 0, hd), q, preferred_element_type=F32)
            z0 = jnp.where(row < col, z0, -jnp.inf)
            z1 = jnp.dot(key_block(top, 1, hd), q, preferred_element_type=F32)
            out.append((z0, z1))
        z_vals[c] = out

    def cumsum01(c):
        d_vals[c] = [cumsum_of(z0) + cumsum_of(z1) for z0, z1 in z_vals[c]]

    def weights01(c):
        a, hg = calls[c]
        top = call_top(a)
        for hl in local:
            hd = hg * ATTN_HEAD_GROUP + hl
            d0, t0, d1, t1 = d_vals[c][hl]
            pv0 = jnp.dot(value_block(top, 0, hd), jnp.exp2(d0).astype(BF16),
                          preferred_element_type=F32)
            pv1 = jnp.dot(value_block(top, 1, hd), jnp.exp2(d1).astype(BF16),
                          preferred_element_type=F32)
            acc_ref[c, hl] = (jnp.where(top >= 0, pv0, 0.0)
                              + jnp.where(top >= 1, pv1 * jnp.exp2(t0), 0.0))
            carry_ref[c, hl] = t0 + t1
            ahead_ref[c, hl] = t0 + t1

    def further_blocks(c):
        a, hg = calls[c]
        top = call_top(a)

        def more_after(m):
            return jnp.logical_and(m + 1 <= top, jnp.max(ahead_ref[c]) >= -EXP2_UNDERFLOW)

        def scores(m, par):
            for hl in local:
                hd = hg * ATTN_HEAD_GROUP + hl
                z_ref[par, hl] = jnp.dot(key_block(top, m, hd), query_tile(a, hd),
                                         preferred_element_type=F32)

        def cumsum(par):
            for hl in local:
                d, t = cumsum_of(z_ref[par, hl])
                d_ref[par, hl] = d
                tot_ref[par, hl] = t
                ahead_ref[c, hl] += t

        def weights(m, par):
            exists = m <= top
            for hl in local:
                hd = hg * ATTN_HEAD_GROUP + hl
                pv = jnp.dot(value_block(top, m, hd), jnp.exp2(d_ref[par, hl]).astype(BF16),
                             preferred_element_type=F32)
                ncarry = carry_ref[c, hl]
                acc_ref[c, hl] += jnp.where(exists, pv * jnp.exp2(ncarry), 0.0)
                carry_ref[c, hl] = ncarry + tot_ref[par, hl]

        @pl.when(more_after(1))
        def _():
            scores(2, 0)
            scores(3, 1)
            cumsum(0)

            def step(t, par):
                scores(t + 2, par)
                cumsum(1 - par)
                weights(t, par)
                return more_after(t + 1)

            def body(state):
                t, _ = state
                more = step(t, 0)
                more_after_odd = lax.cond(more, lambda: step(t + 1, 1), lambda: jnp.bool_(False))
                return jnp.where(more, t + 2, t + 1), more_after_odd

            last, _ = lax.while_loop(lambda state: state[1], body, (jnp.int32(2), True))

            @pl.when(jnp.bitwise_and(last, 1) == 0)
            def _():
                weights(last, 0)

            @pl.when(jnp.bitwise_and(last, 1) == 1)
            def _():
                weights(last, 1)

    hs = [norm(g) for g in groups]
    qkv(0, hs[0])
    qkv(1, hs[1])
    scores01(0)
    conv3_0 = conv_inputs(hs[0])
    cumsum01(0)
    scores01(1)
    conv3_1 = conv_inputs(hs[1])
    weights01(0)
    cumsum01(1)
    y_conv = [conv_branch(0, conv3_0), conv_branch(1, conv3_1)]
    ubuf_ref[0:SUBLANES, :] = ubuf_ref[tm:tm + SUBLANES, :]
    scores01(2)
    weights01(1)
    gate_attn(0, hs[0])
    cumsum01(2)
    scores01(3)
    gate_attn(1, hs[1])
    weights01(2)
    cumsum01(3)
    gate_conv(0, hs[0], y_conv[0])
    weights01(3)
    gate_conv(1, hs[1], y_conv[1])

    last_top = call_top(tm // bk - 1)
    @pl.when(jnp.logical_and(2 <= last_top, jnp.max(ahead_ref[...]) >= -EXP2_UNDERFLOW))
    def _():
        for c in range(len(calls)):
            further_blocks(c)

    for c, (a, hg) in enumerate(calls):
        for hl in local:
            o_ref[0, hg * ATTN_HEAD_GROUP + hl, :, a * bk:(a + 1) * bk] = acc_ref[c, hl].astype(BF16)


def _front_call(x, g, w_t, w_n, b_gate, w_conv, w_conv_out):
    bsz, seq, _ = x.shape
    tm = TOKEN_TILE
    n_tiles = seq // tm
    n_calls = (tm // KEY_BLOCK) * (N_HEADS // ATTN_HEAD_GROUP)
    q_scale = (HEAD_DIM ** -0.5) * math.log2(math.e)
    out_shape = (
        jax.ShapeDtypeStruct((bsz, seq, D_MODEL), BF16),
        jax.ShapeDtypeStruct((bsz, seq, D_MODEL), BF16),
        jax.ShapeDtypeStruct((bsz, N_HEADS, HEAD_DIM, seq), BF16),
    )
    return pl.pallas_call(
        functools.partial(_front_kernel, q_scale=q_scale),
        out_shape=out_shape,
        grid=(bsz, n_tiles),
        in_specs=[
            pl.BlockSpec((1, tm, D_MODEL), lambda b, i: (b, i, 0)),
            _resident((1, D_MODEL)),
            _resident((2 * ATTN_WIDTH, D_MODEL)),
            _resident((D_MODEL, w_n.shape[1])),
            _resident((1, 2 * D_MODEL)),
            _resident((CONV_K, CONV_WIDTH)),
            _resident((CONV_WIDTH, D_MODEL)),
        ],
        out_specs=(
            pl.BlockSpec((1, tm, D_MODEL), lambda b, i: (b, i, 0)),
            pl.BlockSpec((1, tm, D_MODEL), lambda b, i: (b, i, 0)),
            pl.BlockSpec((1, N_HEADS, HEAD_DIM, tm), lambda b, i: (b, 0, 0, i)),
        ),
        scratch_shapes=[
            pltpu.VMEM((tm + SUBLANES, CONV_WIDTH), F32),
            pltpu.VMEM((seq, ATTN_WIDTH), BF16),
            pltpu.VMEM((N_HEADS, seq // KEY_BLOCK, HEAD_DIM, KEY_BLOCK), BF16),
            pltpu.VMEM((N_HEADS, HEAD_PAD, tm), BF16),
            pltpu.VMEM((2, ATTN_HEAD_GROUP, KEY_BLOCK, KEY_BLOCK), F32),
            pltpu.VMEM((2, ATTN_HEAD_GROUP, KEY_BLOCK, KEY_BLOCK), F32),
            pltpu.VMEM((2, ATTN_HEAD_GROUP, 1, KEY_BLOCK), F32),
            pltpu.VMEM((n_calls, ATTN_HEAD_GROUP, 1, KEY_BLOCK), F32),
            pltpu.VMEM((n_calls, ATTN_HEAD_GROUP, 1, KEY_BLOCK), F32),
            pltpu.VMEM((n_calls, ATTN_HEAD_GROUP, HEAD_DIM, KEY_BLOCK), F32),
        ],
        compiler_params=pltpu.CompilerParams(
            dimension_semantics=("arbitrary", "arbitrary"),
            vmem_limit_bytes=VMEM_LIMIT_BYTES),
        name="front_proj_attn",
    )(x, g, w_t, w_n, b_gate, w_conv, w_conv_out)


def _post_kernel(x_ref, ot_ref, gate_ref, gconv_ref, p_ref,
                 wao_ref, wo_ref, wup_ref, wdn_ref, wpg_ref, wpp_ref,
                 g1_ref, g2_ref, g3_ref, g4_ref, out_ref):
    tm = x_ref.shape[1]
    groups = [pl.ds(r * (tm // ROW_GROUPS), tm // ROW_GROUPS) for r in range(ROW_GROUPS)]

    def each(fn, *lists):
        return [fn(*args) for args in zip(*lists)]

    y_attn = each(lambda rows: lax.dot_general(
        ot_ref[0, :, rows], wao_ref[...], (((0,), (0,)), ((), ())),
        preferred_element_type=F32), groups)
    mixed = each(lambda rows, y: jnp.dot(
        (gate_ref[0, rows, :] * y + gconv_ref[0, rows, :]).astype(BF16), wo_ref[...],
        preferred_element_type=F32), groups, y_attn)
    x1 = each(lambda rows, m: x_ref[0, rows, :] + _rms_norm(m, g1_ref[...]), groups, mixed)

    h2 = each(lambda v: _rms_norm(v, g2_ref[...]).astype(BF16), x1)
    up = each(lambda h: jnp.dot(h, wup_ref[...], preferred_element_type=F32), h2)
    f = each(lambda u: jnp.dot(jnp.square(jnp.maximum(u, 0.0)).astype(BF16), wdn_ref[...],
                               preferred_element_type=F32), up)
    x2 = each(lambda v, fv: v + _rms_norm(fv, g3_ref[...]), x1, f)

    gate = each(lambda v: _sigmoid(jnp.dot(_rms_norm(v, g4_ref[...]).astype(BF16), wpg_ref[...],
                                           preferred_element_type=F32)), x2)
    proj = each(lambda rows: jnp.dot(p_ref[0, rows, :].astype(BF16), wpp_ref[...],
                                     preferred_element_type=F32), groups)
    for rows, v, gt, pj in zip(groups, x2, gate, proj):
        out_ref[0, rows, :] = v + gt * pj


def _post_call(x, ot, gate, gconv, p, w_ao, w_o, w_up, w_dn, w_pg, w_pp, g1, g2, g3, g4):
    bsz, seq, _ = x.shape
    tm = TOKEN_TILE
    tok = lambda width: pl.BlockSpec((1, tm, width), lambda b, i: (b, i, 0))
    return pl.pallas_call(
        _post_kernel,
        out_shape=jax.ShapeDtypeStruct((bsz, seq, D_MODEL), F32),
        grid=(bsz, seq // tm),
        in_specs=[
            tok(D_MODEL),
            pl.BlockSpec((1, ATTN_WIDTH, tm), lambda b, i: (b, 0, i)),
            tok(D_MODEL),
            tok(D_MODEL),
            tok(PLE_DIM),
            _resident((ATTN_WIDTH, D_MODEL)),
            _resident((D_MODEL, D_MODEL)),
            _resident((D_MODEL, D_FF)),
            _resident((D_FF, D_MODEL)),
            _resident((D_MODEL, D_MODEL)),
            _resident((PLE_DIM, D_MODEL)),
            _resident((1, D_MODEL)),
            _resident((1, D_MODEL)),
            _resident((1, D_MODEL)),
            _resident((1, D_MODEL)),
        ],
        out_specs=tok(D_MODEL),
        compiler_params=pltpu.CompilerParams(
            dimension_semantics=("arbitrary", "arbitrary"),
            vmem_limit_bytes=VMEM_LIMIT_BYTES),
        name="post_mix_mlp",
    )(x, ot, gate, gconv, p, w_ao, w_o, w_up, w_dn, w_pg, w_pp, g1, g2, g3, g4)


def kernel(x, p, g_pre_mix, w_in, b_gate, w_conv, w_attn_out, w_conv_out, w_o,
           g_post_mix, g_pre_mlp, w_up, w_down, g_post_mlp, g_ple, w_ple_gate, w_ple_proj):
    bsz, seq, _ = x.shape
    depth = w_in.shape[0]
    row = lambda v: v.reshape(1, -1)
    for i in range(depth):
        w = w_in[i].astype(BF16)
        w_t = jnp.concatenate([w[:, 0:ATTN_WIDTH], w[:, 2 * ATTN_WIDTH:3 * ATTN_WIDTH]], axis=1).T
        w_n = jnp.concatenate([w[:, ATTN_WIDTH:2 * ATTN_WIDTH], w[:, 3 * ATTN_WIDTH:]], axis=1)
        gate, gconv, ot = _front_call(
            x, row(g_pre_mix[i]), w_t, w_n, row(b_gate[i]), w_conv[i],
            w_conv_out[i].astype(BF16))
        ot = ot.reshape(bsz, ATTN_WIDTH, seq)
        x = _post_call(
            x, ot, gate, gconv, p[i],
            w_attn_out[i].astype(BF16), w_o[i].astype(BF16), w_up[i].astype(BF16),
            w_down[i].astype(BF16), w_ple_gate[i].astype(BF16), w_ple_proj[i].astype(BF16),
            row(g_post_mix[i]), row(g_pre_mlp[i]), row(g_post_mlp[i]), row(g_ple[i]))
    return x
```

```python
import functools
import math

import jax
import jax.numpy as jnp
from jax import lax
from jax.experimental import pallas as pl
from jax.experimental.pallas import tpu as pltpu

D_MODEL = 1024
N_HEADS = 8
HEAD_DIM = 64
ATTN_WIDTH = N_HEADS * HEAD_DIM
CONV_WIDTH = 512
CONV_K = 3
D_FF = 4 * D_MODEL
PLE_DIM = 256
EPS = 1e-6

LANES = 128
SUBLANES = 8
VMEM_LIMIT_BYTES = 56 * 1024 * 1024

TOKEN_TILE = 512
ROW_GROUPS = 2
KEY_BLOCK = 256
HEAD_PAD = 2 * HEAD_DIM
ATTN_HEAD_GROUP = 4
EXP2_UNDERFLOW = 160.0

BF16 = jnp.bfloat16
F32 = jnp.float32


def _rms_norm(xf, g):
    var = jnp.mean(xf * xf, axis=-1, keepdims=True)
    return xf * lax.rsqrt(var + EPS) * g


def _sigmoid(v):
    return 1.0 / (1.0 + jnp.exp(-v))


def _resident(shape):
    zeros = (0,) * len(shape)
    return pl.BlockSpec(shape, lambda *_: zeros, pipeline_mode=pl.Buffered(1))


def _softplus2(z):
    return jnp.maximum(z, 0) + jnp.log2(1 + jnp.exp2(-jnp.abs(z)))


def _front_kernel(x_ref, g_ref, wt_ref, wn_ref, bg_ref, wc_ref, wco_ref,
                  gate_ref, gconv_ref, o_ref,
                  ubuf_ref, kall_ref, vall_ref, q_ref,
                  z_ref, d_ref, tot_ref, carry_ref, ahead_ref, acc_ref, *, q_scale, n_tiles):
    tm = x_ref.shape[1]
    bk = KEY_BLOCK
    tg = bk
    groups = range(tm // tg)
    seq = n_tiles * tm
    t = pl.program_id(0)
    last = pl.num_programs(0) - 2
    tp = jnp.minimum(t, last)
    batch, i = tp // n_tiles, tp % n_tiles
    ta = jnp.maximum(t - 1, 0)
    batch_a, i_a = ta // n_tiles, ta % n_tiles
    q_write = jnp.bitwise_and(t, 1)
    first_row = jnp.where(jnp.bitwise_and(batch, 1) == 0, seq, 0)
    first_row_a = jnp.where(jnp.bitwise_and(batch_a, 1) == 0, seq, 0)
    c0 = ATTN_WIDTH
    g0 = ATTN_WIDTH + 3 * CONV_WIDTH

    row = lax.broadcasted_iota(jnp.int32, (bk, bk), 0)
    col = lax.broadcasted_iota(jnp.int32, (bk, bk), 1)
    neg_tri = jnp.where(col >= row, -1.0, 0.0).astype(BF16)

    @pl.when(i == 0)
    def _():
        ubuf_ref[0:SUBLANES, :] = jnp.zeros((SUBLANES, CONV_WIDTH), F32)

    @pl.when(t == 0)
    def _():
        q_ref[1] = jnp.zeros(q_ref.shape[1:], BF16)
        kall_ref[seq:seq + bk, :] = jnp.zeros((bk, ATTN_WIDTH), BF16)
        vall_ref[:, seq // bk] = jnp.zeros((N_HEADS, HEAD_DIM, bk), BF16)

    def norm(g):
        return _rms_norm(x_ref[0, pl.ds(g * tg, tg), :], g_ref[...]).astype(BF16)

    def qkv(g, h):
        cols = pl.ds(g * tg, tg)
        start = pl.multiple_of(jnp.where(i == 0, first_row, i * tm) + g * tg, tg)
        tr = lax.dot_general(wt_ref[...], h, (((1,), (1,)), ((), ())),
                             preferred_element_type=F32)
        zeros_half = jnp.zeros((HEAD_DIM, tg), BF16)
        for hd in range(N_HEADS):
            q_h = (tr[hd * HEAD_DIM:(hd + 1) * HEAD_DIM] * q_scale).astype(BF16)
            lo = (hd % 2) * HEAD_DIM
            q_ref[q_write, hd, lo:lo + HEAD_DIM, cols] = q_h
            q_ref[q_write, hd, HEAD_DIM - lo:2 * HEAD_DIM - lo, cols] = zeros_half
            vall_ref[hd, start // bk] = tr[ATTN_WIDTH + hd * HEAD_DIM:
                                           ATTN_WIDTH + (hd + 1) * HEAD_DIM].astype(BF16)
        kall_ref[pl.ds(start, tg), :] = jnp.dot(h, wn_ref[:, 0:ATTN_WIDTH],
                                                preferred_element_type=F32).astype(BF16)

    def conv_inputs(h):
        return jnp.dot(h, wn_ref[:, c0:c0 + 3 * CONV_WIDTH], preferred_element_type=F32)

    def conv_branch(g, conv3):
        r0 = SUBLANES + g * tg
        cb = conv3[:, 0:CONV_WIDTH]
        u = conv3[:, CONV_WIDTH:2 * CONV_WIDTH] * conv3[:, 2 * CONV_WIDTH:3 * CONV_WIDTH]
        ubuf_ref[r0:r0 + tg, :] = u
        u1 = ubuf_ref[r0 - 1:r0 - 1 + tg, :]
        u2 = ubuf_ref[r0 - 2:r0 - 2 + tg, :]
        wc = wc_ref[...]
        conv = wc[0:1, :] * u2 + wc[1:2, :] * u1 + wc[2:3, :] * u
        return jnp.dot((cb * conv).astype(BF16), wco_ref[...],
                       preferred_element_type=F32)

    def gate_attn(g, h):
        ga = jnp.dot(h, wn_ref[:, g0:g0 + D_MODEL], preferred_element_type=F32)
        gate_ref[pl.ds(g * tg, tg), :] = _sigmoid(ga + bg_ref[:, 0:D_MODEL]).astype(BF16)

    def gate_conv(g, h, y_conv):
        gc = jnp.dot(h, wn_ref[:, g0 + D_MODEL:g0 + 2 * D_MODEL], preferred_element_type=F32)
        gconv_ref[pl.ds(g * tg, tg), :] = (_sigmoid(gc + bg_ref[:, D_MODEL:2 * D_MODEL])
                                           * y_conv).astype(BF16)

    calls = [(a, hg) for a in range(tm // bk) for hg in range(N_HEADS // ATTN_HEAD_GROUP)]
    local = range(ATTN_HEAD_GROUP)

    def call_top(a):
        return jnp.where(t == 0, -1, (tm // bk) * i_a + a)

    def block_row(top, m):
        blk = jnp.maximum(top - m, 0)
        return pl.multiple_of(jnp.where(blk < tm // bk, first_row_a, 0) + blk * bk, bk)

    def key_block(top, m, hd):
        pair = (hd // 2) * HEAD_PAD
        return kall_ref[pl.ds(block_row(top, m), bk), pair:pair + HEAD_PAD]

    def value_block(top, m, hd):
        return vall_ref[hd, block_row(top, m) // bk]

    def query_tile(a, hd):
        return q_ref[1 - q_write, hd, :, a * bk:(a + 1) * bk]

    def cumsum_of(z):
        ncsum = jnp.dot(neg_tri, _softplus2(z).astype(BF16), preferred_element_type=F32)
        return ncsum + z, ncsum[0:1, :]

    z_vals, d_vals = {}, {}

    def scores01(c):
        a, hg = calls[c]
        top = call_top(a)
        out = []
        for hl in local:
            hd = hg * ATTN_HEAD_GROUP + hl
            q = query_tile(a, hd)
            z0 = jnp.dot(key_block(top, 0, hd), q, preferred_element_type=F32)
            z0 = jnp.where(row < col, z0, -jnp.inf)
            z1 = jnp.dot(key_block(top, 1, hd), q, preferred_element_type=F32)
            out.append((z0, z1))
        z_vals[c] = out

    def cumsum01(c):
        d_vals[c] = [cumsum_of(z0) + cumsum_of(z1) for z0, z1 in z_vals[c]]

    def weights01(c):
        a, hg = calls[c]
        top = call_top(a)
        for hl in local:
            hd = hg * ATTN_HEAD_GROUP + hl
            d0, t0, d1, t1 = d_vals[c][hl]
            pv0 = jnp.dot(value_block(top, 0, hd), jnp.exp2(d0).astype(BF16),
                          preferred_element_type=F32)
            pv1 = jnp.dot(value_block(top, 1, hd), jnp.exp2(d1).astype(BF16),
                          preferred_element_type=F32)
            acc_ref[c, hl] = (jnp.where(top >= 0, pv0, 0.0)
                              + jnp.where(top >= 1, pv1 * jnp.exp2(t0), 0.0))
            carry_ref[c, hl] = t0 + t1
            ahead_ref[c, hl] = t0 + t1

    def further_blocks(c):
        a, hg = calls[c]
        top = call_top(a)

        def more_after(m):
            return jnp.logical_and(m + 1 <= top, jnp.max(ahead_ref[c]) >= -EXP2_UNDERFLOW)

        def scores(m, par):
            for hl in local:
                hd = hg * ATTN_HEAD_GROUP + hl
                z_ref[par, hl] = jnp.dot(key_block(top, m, hd), query_tile(a, hd),
                                         preferred_element_type=F32)

        def cumsum(par):
            for hl in local:
                d, t = cumsum_of(z_ref[par, hl])
                d_ref[par, hl] = d
                tot_ref[par, hl] = t
                ahead_ref[c, hl] += t

        def weights(m, par):
            exists = m <= top
            for hl in local:
                hd = hg * ATTN_HEAD_GROUP + hl
                pv = jnp.dot(value_block(top, m, hd), jnp.exp2(d_ref[par, hl]).astype(BF16),
                             preferred_element_type=F32)
                ncarry = carry_ref[c, hl]
                acc_ref[c, hl] += jnp.where(exists, pv * jnp.exp2(ncarry), 0.0)
                carry_ref[c, hl] = ncarry + tot_ref[par, hl]

        @pl.when(more_after(1))
        def _():
            scores(2, 0)
            scores(3, 1)
            cumsum(0)

            def step(t, par):
                scores(t + 2, par)
                cumsum(1 - par)
                weights(t, par)
                return more_after(t + 1)

            def body(state):
                t, _ = state
                more = step(t, 0)
                more_after_odd = lax.cond(more, lambda: step(t + 1, 1), lambda: jnp.bool_(False))
                return jnp.where(more, t + 2, t + 1), more_after_odd

            last, _ = lax.while_loop(lambda state: state[1], body, (jnp.int32(2), True))

            @pl.when(jnp.bitwise_and(last, 1) == 0)
            def _():
                weights(last, 0)

            @pl.when(jnp.bitwise_and(last, 1) == 1)
            def _():
                weights(last, 1)

    hs = [norm(g) for g in groups]
    scores01(0)
    qkv(0, hs[0])
    cumsum01(0)
    qkv(1, hs[1])
    weights01(0)
    scores01(1)
    conv3_0 = conv_inputs(hs[0])
    cumsum01(1)
    conv3_1 = conv_inputs(hs[1])
    weights01(1)
    scores01(2)
    y_conv = [conv_branch(0, conv3_0), conv_branch(1, conv3_1)]
    ubuf_ref[0:SUBLANES, :] = ubuf_ref[tm:tm + SUBLANES, :]
    cumsum01(2)
    gate_attn(0, hs[0])
    weights01(2)
    scores01(3)
    gate_attn(1, hs[1])
    cumsum01(3)
    gate_conv(0, hs[0], y_conv[0])
    weights01(3)
    gate_conv(1, hs[1], y_conv[1])

    last_top = call_top(tm // bk - 1)
    @pl.when(jnp.logical_and(2 <= last_top, jnp.max(ahead_ref[...]) >= -EXP2_UNDERFLOW))
    def _():
        for c in range(len(calls)):
            further_blocks(c)

    for c, (a, hg) in enumerate(calls):
        for hl in local:
            o_ref[0, hg * ATTN_HEAD_GROUP + hl, :, a * bk:(a + 1) * bk] = acc_ref[c, hl].astype(BF16)


def _front_call(x, g, w_t, w_n, b_gate, w_conv, w_conv_out):
    bsz, seq, _ = x.shape
    tm = TOKEN_TILE
    n_tiles = seq // tm
    total = bsz * n_tiles
    n_calls = (tm // KEY_BLOCK) * (N_HEADS // ATTN_HEAD_GROUP)
    q_scale = (HEAD_DIM ** -0.5) * math.log2(math.e)
    out_shape = (
        jax.ShapeDtypeStruct((bsz * seq + tm, D_MODEL), BF16),
        jax.ShapeDtypeStruct((bsz * seq + tm, D_MODEL), BF16),
        jax.ShapeDtypeStruct((bsz, N_HEADS, HEAD_DIM, seq), BF16),
    )

    def projected(t):
        tp = jnp.minimum(t, total - 1)
        return tp // n_tiles, tp % n_tiles

    def attended(t):
        ta = jnp.maximum(t - 1, 0)
        return ta // n_tiles, ta % n_tiles

    return pl.pallas_call(
        functools.partial(_front_kernel, q_scale=q_scale, n_tiles=n_tiles),
        out_shape=out_shape,
        grid=(total + 1,),
        in_specs=[
            pl.BlockSpec((1, tm, D_MODEL), lambda t: (*projected(t), 0)),
            _resident((1, D_MODEL)),
            _resident((2 * ATTN_WIDTH, D_MODEL)),
            _resident((D_MODEL, w_n.shape[1])),
            _resident((1, 2 * D_MODEL)),
            _resident((CONV_K, CONV_WIDTH)),
            _resident((CONV_WIDTH, D_MODEL)),
        ],
        out_specs=(
            pl.BlockSpec((tm, D_MODEL), lambda t: (t, 0)),
            pl.BlockSpec((tm, D_MODEL), lambda t: (t, 0)),
            pl.BlockSpec((1, N_HEADS, HEAD_DIM, tm),
                         lambda t: (attended(t)[0], 0, 0, attended(t)[1])),
        ),
        scratch_shapes=[
            pltpu.VMEM((tm + SUBLANES, CONV_WIDTH), F32),
            pltpu.VMEM((seq + tm, ATTN_WIDTH), BF16),
            pltpu.VMEM((N_HEADS, (seq + tm) // KEY_BLOCK, HEAD_DIM, KEY_BLOCK), BF16),
            pltpu.VMEM((2, N_HEADS, HEAD_PAD, tm), BF16),
            pltpu.VMEM((2, ATTN_HEAD_GROUP, KEY_BLOCK, KEY_BLOCK), F32),
            pltpu.VMEM((2, ATTN_HEAD_GROUP, KEY_BLOCK, KEY_BLOCK), F32),
            pltpu.VMEM((2, ATTN_HEAD_GROUP, 1, KEY_BLOCK), F32),
            pltpu.VMEM((n_calls, ATTN_HEAD_GROUP, 1, KEY_BLOCK), F32),
            pltpu.VMEM((n_calls, ATTN_HEAD_GROUP, 1, KEY_BLOCK), F32),
            pltpu.VMEM((n_calls, ATTN_HEAD_GROUP, HEAD_DIM, KEY_BLOCK), F32),
        ],
        compiler_params=pltpu.CompilerParams(
            dimension_semantics=("arbitrary",),
            vmem_limit_bytes=VMEM_LIMIT_BYTES),
        name="front_proj_attn",
    )(x, g, w_t, w_n, b_gate, w_conv, w_conv_out)


def _post_kernel(x_ref, ot_ref, gate_ref, gconv_ref, p_ref,
                 wao_ref, wo_ref, wup_ref, wdn_ref, wpg_ref, wpp_ref,
                 g1_ref, g2_ref, g3_ref, g4_ref, out_ref):
    tm = x_ref.shape[1]
    groups = [pl.ds(r * (tm // ROW_GROUPS), tm // ROW_GROUPS) for r in range(ROW_GROUPS)]

    def each(fn, *lists):
        return [fn(*args) for args in zip(*lists)]

    y_attn = each(lambda rows: lax.dot_general(
        ot_ref[0, :, rows], wao_ref[...], (((0,), (0,)), ((), ())),
        preferred_element_type=F32), groups)
    mixed = each(lambda rows, y: jnp.dot(
        (gate_ref[rows, :] * y + gconv_ref[rows, :]).astype(BF16), wo_ref[...],
        preferred_element_type=F32), groups, y_attn)
    x1 = each(lambda rows, m: x_ref[0, rows, :] + _rms_norm(m, g1_ref[...]), groups, mixed)

    h2 = each(lambda v: _rms_norm(v, g2_ref[...]).astype(BF16), x1)
    up = each(lambda h: jnp.dot(h, wup_ref[...], preferred_element_type=F32), h2)
    f = each(lambda u: jnp.dot(jnp.square(jnp.maximum(u, 0.0)).astype(BF16), wdn_ref[...],
                               preferred_element_type=F32), up)
    x2 = each(lambda v, fv: v + _rms_norm(fv, g3_ref[...]), x1, f)

    gate = each(lambda v: _sigmoid(jnp.dot(_rms_norm(v, g4_ref[...]).astype(BF16), wpg_ref[...],
                                           preferred_element_type=F32)), x2)
    proj = each(lambda rows: jnp.dot(p_ref[0, rows, :].astype(BF16), wpp_ref[...],
                                     preferred_element_type=F32), groups)
    for rows, v, gt, pj in zip(groups, x2, gate, proj):
        out_ref[0, rows, :] = v + gt * pj


def _post_call(x, ot, gate, gconv, p, w_ao, w_o, w_up, w_dn, w_pg, w_pp, g1, g2, g3, g4):
    bsz, seq, _ = x.shape
    tm = TOKEN_TILE
    n_tiles = seq // tm
    tok = lambda width: pl.BlockSpec((1, tm, width), lambda b, i: (b, i, 0))
    flat = pl.BlockSpec((tm, D_MODEL), lambda b, i: (b * n_tiles + i, 0))
    return pl.pallas_call(
        _post_kernel,
        out_shape=jax.ShapeDtypeStruct((bsz, seq, D_MODEL), F32),
        grid=(bsz, seq // tm),
        in_specs=[
            tok(D_MODEL),
            pl.BlockSpec((1, ATTN_WIDTH, tm), lambda b, i: (b, 0, i)),
            flat,
            flat,
            tok(PLE_DIM),
            _resident((ATTN_WIDTH, D_MODEL)),
            _resident((D_MODEL, D_MODEL)),
            _resident((D_MODEL, D_FF)),
            _resident((D_FF, D_MODEL)),
            _resident((D_MODEL, D_MODEL)),
            _resident((PLE_DIM, D_MODEL)),
            _resident((1, D_MODEL)),
            _resident((1, D_MODEL)),
            _resident((1, D_MODEL)),
            _resident((1, D_MODEL)),
        ],
        out_specs=tok(D_MODEL),
        compiler_params=pltpu.CompilerParams(
            dimension_semantics=("arbitrary", "arbitrary"),
            vmem_limit_bytes=VMEM_LIMIT_BYTES),
        name="post_mix_mlp",
    )(x, ot, gate, gconv, p, w_ao, w_o, w_up, w_dn, w_pg, w_pp, g1, g2, g3, g4)


def kernel(x, p, g_pre_mix, w_in, b_gate, w_conv, w_attn_out, w_conv_out, w_o,
           g_post_mix, g_pre_mlp, w_up, w_down, g_post_mlp, g_ple, w_ple_gate, w_ple_proj):
    bsz, seq, _ = x.shape
    depth = w_in.shape[0]
    row = lambda v: v.reshape(1, -1)
    for i in range(depth):
        w = w_in[i].astype(BF16)
        w_t = jnp.concatenate([w[:, 0:ATTN_WIDTH], w[:, 2 * ATTN_WIDTH:3 * ATTN_WIDTH]], axis=1).T
        w_n = jnp.concatenate([w[:, ATTN_WIDTH:2 * ATTN_WIDTH], w[:, 3 * ATTN_WIDTH:]], axis=1)
        gate, gconv, ot = _front_call(
            x, row(g_pre_mix[i]), w_t, w_n, row(b_gate[i]), w_conv[i],
            w_conv_out[i].astype(BF16))
        ot = ot.reshape(bsz, ATTN_WIDTH, seq)
        x = _post_call(
            x, ot, gate, gconv, p[i],
            w_attn_out[i].astype(BF16), w_o[i].astype(BF16), w_up[i].astype(BF16),
            w_down[i].astype(BF16), w_ple_gate[i].astype(BF16), w_ple_proj[i].astype(BF16),
            row(g_post_mix[i]), row(g_pre_mlp[i]), row(g_post_mlp[i]), row(g_ple[i]))
    return x
```

```python
import functools
import math

import jax
import jax.numpy as jnp
from jax import lax
from jax.experimental import pallas as pl
from jax.experimental.pallas import tpu as pltpu

D_MODEL = 1024
N_HEADS = 8
HEAD_DIM = 64
ATTN_WIDTH = N_HEADS * HEAD_DIM
CONV_WIDTH = 512
CONV_K = 3
D_FF = 4 * D_MODEL
PLE_DIM = 256
EPS = 1e-6

LANES = 128
SUBLANES = 8
VMEM_LIMIT_BYTES = 56 * 1024 * 1024

TOKEN_TILE = 512
ROW_GROUPS = 2
KEY_BLOCK = 256
HEAD_PAD = 2 * HEAD_DIM
ATTN_HEAD_GROUP = 4
EXP2_UNDERFLOW = 160.0

BF16 = jnp.bfloat16
F32 = jnp.float32


def _rms_norm(xf, g):
    var = jnp.mean(xf * xf, axis=-1, keepdims=True)
    return xf * lax.rsqrt(var + EPS) * g


def _sigmoid(v):
    return 1.0 / (1.0 + jnp.exp(-v))


def _resident(shape):
    zeros = (0,) * len(shape)
    return pl.BlockSpec(shape, lambda *_: zeros, pipeline_mode=pl.Buffered(1))


def _softplus2(z):
    return jnp.maximum(z, 0) + jnp.log2(1 + jnp.exp2(-jnp.abs(z)))


def _front_kernel(x_ref, g_ref, w_ref, bg_ref, wc_ref, wco_ref,
                  gate_ref, gconv_ref, o_ref,
                  wt_ref, ubuf_ref, kall_ref, vall_ref, q_ref,
                  z_ref, d_ref, tot_ref, carry_ref, ahead_ref, acc_ref, *, q_scale, n_tiles):
    tm = x_ref.shape[1]
    bk = KEY_BLOCK
    tg = bk
    groups = range(tm // tg)
    seq = n_tiles * tm
    t = pl.program_id(0)
    last = pl.num_programs(0) - 2
    tp = jnp.minimum(t, last)
    batch, i = tp // n_tiles, tp % n_tiles
    ta = jnp.maximum(t - 1, 0)
    batch_a, i_a = ta // n_tiles, ta % n_tiles
    q_write = jnp.bitwise_and(t, 1)
    first_row = jnp.where(jnp.bitwise_and(batch, 1) == 0, seq, 0)
    first_row_a = jnp.where(jnp.bitwise_and(batch_a, 1) == 0, seq, 0)
    k0, v0, c0 = ATTN_WIDTH, 2 * ATTN_WIDTH, 3 * ATTN_WIDTH
    g0 = c0 + 3 * CONV_WIDTH

    row = lax.broadcasted_iota(jnp.int32, (bk, bk), 0)
    col = lax.broadcasted_iota(jnp.int32, (bk, bk), 1)
    neg_tri = jnp.where(col >= row, -1.0, 0.0).astype(BF16)

    @pl.when(i == 0)
    def _():
        ubuf_ref[0:SUBLANES, :] = jnp.zeros((SUBLANES, CONV_WIDTH), F32)

    @pl.when(t == 0)
    def _():
        q_ref[1] = jnp.zeros(q_ref.shape[1:], BF16)
        wt_ref[0:ATTN_WIDTH, :] = w_ref[:, 0:ATTN_WIDTH].T
        wt_ref[ATTN_WIDTH:2 * ATTN_WIDTH, :] = w_ref[:, v0:v0 + ATTN_WIDTH].T
        kall_ref[seq:seq + bk, :] = jnp.zeros((bk, ATTN_WIDTH), BF16)
        vall_ref[:, seq // bk] = jnp.zeros((N_HEADS, HEAD_DIM, bk), BF16)

    def norm(g):
        return _rms_norm(x_ref[0, pl.ds(g * tg, tg), :], g_ref[...]).astype(BF16)

    def qkv(g, h):
        cols = pl.ds(g * tg, tg)
        start = pl.multiple_of(jnp.where(i == 0, first_row, i * tm) + g * tg, tg)
        tr = lax.dot_general(wt_ref[...], h, (((1,), (1,)), ((), ())),
                             preferred_element_type=F32)
        zeros_half = jnp.zeros((HEAD_DIM, tg), BF16)
        for hd in range(N_HEADS):
            q_h = (tr[hd * HEAD_DIM:(hd + 1) * HEAD_DIM] * q_scale).astype(BF16)
            lo = (hd % 2) * HEAD_DIM
            q_ref[q_write, hd, lo:lo + HEAD_DIM, cols] = q_h
            q_ref[q_write, hd, HEAD_DIM - lo:2 * HEAD_DIM - lo, cols] = zeros_half
            vall_ref[hd, start // bk] = tr[ATTN_WIDTH + hd * HEAD_DIM:
                                           ATTN_WIDTH + (hd + 1) * HEAD_DIM].astype(BF16)
        kall_ref[pl.ds(start, tg), :] = jnp.dot(h, w_ref[:, k0:k0 + ATTN_WIDTH],
                                                preferred_element_type=F32).astype(BF16)

    def conv_inputs(h):
        return jnp.dot(h, w_ref[:, c0:c0 + 3 * CONV_WIDTH], preferred_element_type=F32)

    def conv_branch(g, conv3):
        r0 = SUBLANES + g * tg
        cb = conv3[:, 0:CONV_WIDTH]
        u = conv3[:, CONV_WIDTH:2 * CONV_WIDTH] * conv3[:, 2 * CONV_WIDTH:3 * CONV_WIDTH]
        ubuf_ref[r0:r0 + tg, :] = u
        u1 = ubuf_ref[r0 - 1:r0 - 1 + tg, :]
        u2 = ubuf_ref[r0 - 2:r0 - 2 + tg, :]
        wc = wc_ref[...]
        conv = wc[0:1, :] * u2 + wc[1:2, :] * u1 + wc[2:3, :] * u
        return jnp.dot((cb * conv).astype(BF16), wco_ref[...],
                       preferred_element_type=F32)

    def gate_attn(g, h):
        ga = jnp.dot(h, w_ref[:, g0:g0 + D_MODEL], preferred_element_type=F32)
        gate_ref[pl.ds(g * tg, tg), :] = _sigmoid(ga + bg_ref[:, 0:D_MODEL]).astype(BF16)

    def gate_conv(g, h, y_conv):
        gc = jnp.dot(h, w_ref[:, g0 + D_MODEL:g0 + 2 * D_MODEL], preferred_element_type=F32)
        gconv_ref[pl.ds(g * tg, tg), :] = (_sigmoid(gc + bg_ref[:, D_MODEL:2 * D_MODEL])
                                           * y_conv).astype(BF16)

    calls = [(a, hg) for a in range(tm // bk) for hg in range(N_HEADS // ATTN_HEAD_GROUP)]
    local = range(ATTN_HEAD_GROUP)

    def call_top(a):
        return jnp.where(t == 0, -1, (tm // bk) * i_a + a)

    def block_row(top, m):
        blk = jnp.maximum(top - m, 0)
        return pl.multiple_of(jnp.where(blk < tm // bk, first_row_a, 0) + blk * bk, bk)

    def key_block(top, m, hd):
        pair = (hd // 2) * HEAD_PAD
        return kall_ref[pl.ds(block_row(top, m), bk), pair:pair + HEAD_PAD]

    def value_block(top, m, hd):
        return vall_ref[hd, block_row(top, m) // bk]

    def query_tile(a, hd):
        return q_ref[1 - q_write, hd, :, a * bk:(a + 1) * bk]

    def cumsum_of(z):
        ncsum = jnp.dot(neg_tri, _softplus2(z).astype(BF16), preferred_element_type=F32)
        return ncsum + z, ncsum[0:1, :]

    z_vals, d_vals = {}, {}

    def scores01(c):
        a, hg = calls[c]
        top = call_top(a)
        out = []
        for hl in local:
            hd = hg * ATTN_HEAD_GROUP + hl
            q = query_tile(a, hd)
            z0 = jnp.dot(key_block(top, 0, hd), q, preferred_element_type=F32)
            z0 = jnp.where(row < col, z0, -jnp.inf)
            z1 = jnp.dot(key_block(top, 1, hd), q, preferred_element_type=F32)
            out.append((z0, z1))
        z_vals[c] = out

    def cumsum01(c):
        d_vals[c] = [cumsum_of(z0) + cumsum_of(z1) for z0, z1 in z_vals[c]]

    def weights01(c):
        a, hg = calls[c]
        top = call_top(a)
        for hl in local:
            hd = hg * ATTN_HEAD_GROUP + hl
            d0, t0, d1, t1 = d_vals[c][hl]
            pv0 = jnp.dot(value_block(top, 0, hd), jnp.exp2(d0).astype(BF16),
                          preferred_element_type=F32)
            pv1 = jnp.dot(value_block(top, 1, hd), jnp.exp2(d1).astype(BF16),
                          preferred_element_type=F32)
            acc_ref[c, hl] = (jnp.where(top >= 0, pv0, 0.0)
                              + jnp.where(top >= 1, pv1 * jnp.exp2(t0), 0.0))
            carry_ref[c, hl] = t0 + t1
            ahead_ref[c, hl] = t0 + t1

    def further_blocks(c):
        a, hg = calls[c]
        top = call_top(a)

        def more_after(m):
            return jnp.logical_and(m + 1 <= top, jnp.max(ahead_ref[c]) >= -EXP2_UNDERFLOW)

        def scores(m, par):
            for hl in local:
                hd = hg * ATTN_HEAD_GROUP + hl
                z_ref[par, hl] = jnp.dot(key_block(top, m, hd), query_tile(a, hd),
                                         preferred_element_type=F32)

        def cumsum(par):
            for hl in local:
                d, t = cumsum_of(z_ref[par, hl])
                d_ref[par, hl] = d
                tot_ref[par, hl] = t
                ahead_ref[c, hl] += t

        def weights(m, par):
            exists = m <= top
            for hl in local:
                hd = hg * ATTN_HEAD_GROUP + hl
                pv = jnp.dot(value_block(top, m, hd), jnp.exp2(d_ref[par, hl]).astype(BF16),
                             preferred_element_type=F32)
                ncarry = carry_ref[c, hl]
                acc_ref[c, hl] += jnp.where(exists, pv * jnp.exp2(ncarry), 0.0)
                carry_ref[c, hl] = ncarry + tot_ref[par, hl]

        @pl.when(more_after(1))
        def _():
            scores(2, 0)
            scores(3, 1)
            cumsum(0)

            def step(t, par):
                scores(t + 2, par)
                cumsum(1 - par)
                weights(t, par)
                return more_after(t + 1)

            def body(state):
                t, _ = state
                more = step(t, 0)
                more_after_odd = lax.cond(more, lambda: step(t + 1, 1), lambda: jnp.bool_(False))
                return jnp.where(more, t + 2, t + 1), more_after_odd

            last, _ = lax.while_loop(lambda state: state[1], body, (jnp.int32(2), True))

            @pl.when(jnp.bitwise_and(last, 1) == 0)
            def _():
                weights(last, 0)

            @pl.when(jnp.bitwise_and(last, 1) == 1)
            def _():
                weights(last, 1)

    hs = [norm(g) for g in groups]
    scores01(0)
    qkv(0, hs[0])
    cumsum01(0)
    qkv(1, hs[1])
    weights01(0)
    scores01(1)
    conv3_0 = conv_inputs(hs[0])
    cumsum01(1)
    conv3_1 = conv_inputs(hs[1])
    weights01(1)
    scores01(2)
    y_conv = [conv_branch(0, conv3_0), conv_branch(1, conv3_1)]
    ubuf_ref[0:SUBLANES, :] = ubuf_ref[tm:tm + SUBLANES, :]
    cumsum01(2)
    gate_attn(0, hs[0])
    weights01(2)
    scores01(3)
    gate_attn(1, hs[1])
    cumsum01(3)
    gate_conv(0, hs[0], y_conv[0])
    weights01(3)
    gate_conv(1, hs[1], y_conv[1])

    last_top = call_top(tm // bk - 1)
    @pl.when(jnp.logical_and(2 <= last_top, jnp.max(ahead_ref[...]) >= -EXP2_UNDERFLOW))
    def _():
        for c in range(len(calls)):
            further_blocks(c)

    for c, (a, hg) in enumerate(calls):
        for hl in local:
            o_ref[0, hg * ATTN_HEAD_GROUP + hl, :, a * bk:(a + 1) * bk] = acc_ref[c, hl].astype(BF16)


def _front_call(x, g, w, b_gate, w_conv, w_conv_out):
    bsz, seq, _ = x.shape
    tm = TOKEN_TILE
    n_tiles = seq // tm
    total = bsz * n_tiles
    n_calls = (tm // KEY_BLOCK) * (N_HEADS // ATTN_HEAD_GROUP)
    q_scale = (HEAD_DIM ** -0.5) * math.log2(math.e)
    out_shape = (
        jax.ShapeDtypeStruct((bsz * seq + tm, D_MODEL), BF16),
        jax.ShapeDtypeStruct((bsz * seq + tm, D_MODEL), BF16),
        jax.ShapeDtypeStruct((bsz, N_HEADS, HEAD_DIM, seq), BF16),
    )

    def projected(t):
        tp = jnp.minimum(t, total - 1)
        return tp // n_tiles, tp % n_tiles

    def attended(t):
        ta = jnp.maximum(t - 1, 0)
        return ta // n_tiles, ta % n_tiles

    return pl.pallas_call(
        functools.partial(_front_kernel, q_scale=q_scale, n_tiles=n_tiles),
        out_shape=out_shape,
        grid=(total + 1,),
        in_specs=[
            pl.BlockSpec((1, tm, D_MODEL), lambda t: (*projected(t), 0)),
            _resident((1, D_MODEL)),
            _resident(w.shape),
            _resident((1, 2 * D_MODEL)),
            _resident((CONV_K, CONV_WIDTH)),
            _resident((CONV_WIDTH, D_MODEL)),
        ],
        out_specs=(
            pl.BlockSpec((tm, D_MODEL), lambda t: (t, 0)),
            pl.BlockSpec((tm, D_MODEL), lambda t: (t, 0)),
            pl.BlockSpec((1, N_HEADS, HEAD_DIM, tm),
                         lambda t: (attended(t)[0], 0, 0, attended(t)[1])),
        ),
        scratch_shapes=[
            pltpu.VMEM((2 * ATTN_WIDTH, D_MODEL), BF16),
            pltpu.VMEM((tm + SUBLANES, CONV_WIDTH), F32),
            pltpu.VMEM((seq + tm, ATTN_WIDTH), BF16),
            pltpu.VMEM((N_HEADS, (seq + tm) // KEY_BLOCK, HEAD_DIM, KEY_BLOCK), BF16),
            pltpu.VMEM((2, N_HEADS, HEAD_PAD, tm), BF16),
            pltpu.VMEM((2, ATTN_HEAD_GROUP, KEY_BLOCK, KEY_BLOCK), F32),
            pltpu.VMEM((2, ATTN_HEAD_GROUP, KEY_BLOCK, KEY_BLOCK), F32),
            pltpu.VMEM((2, ATTN_HEAD_GROUP, 1, KEY_BLOCK), F32),
            pltpu.VMEM((n_calls, ATTN_HEAD_GROUP, 1, KEY_BLOCK), F32),
            pltpu.VMEM((n_calls, ATTN_HEAD_GROUP, 1, KEY_BLOCK), F32),
            pltpu.VMEM((n_calls, ATTN_HEAD_GROUP, HEAD_DIM, KEY_BLOCK), F32),
        ],
        compiler_params=pltpu.CompilerParams(
            dimension_semantics=("arbitrary",),
            vmem_limit_bytes=VMEM_LIMIT_BYTES),
        name="front_proj_attn",
    )(x, g, w, b_gate, w_conv, w_conv_out)


def _post_kernel(x_ref, ot_ref, gate_ref, gconv_ref, p_ref,
                 wao_ref, wo_ref, wup_ref, wdn_ref, wpg_ref, wpp_ref,
                 g1_ref, g2_ref, g3_ref, g4_ref, out_ref):
    tm = x_ref.shape[1]
    groups = [pl.ds(r * (tm // ROW_GROUPS), tm // ROW_GROUPS) for r in range(ROW_GROUPS)]

    def each(fn, *lists):
        return [fn(*args) for args in zip(*lists)]

    y_attn = each(lambda rows: lax.dot_general(
        ot_ref[0, :, rows], wao_ref[...], (((0,), (0,)), ((), ())),
        preferred_element_type=F32), groups)
    mixed = each(lambda rows, y: jnp.dot(
        (gate_ref[rows, :] * y + gconv_ref[rows, :]).astype(BF16), wo_ref[...],
        preferred_element_type=F32), groups, y_attn)
    x1 = each(lambda rows, m: x_ref[0, rows, :] + _rms_norm(m, g1_ref[...]), groups, mixed)

    h2 = each(lambda v: _rms_norm(v, g2_ref[...]).astype(BF16), x1)
    up = each(lambda h: jnp.dot(h, wup_ref[...], preferred_element_type=F32), h2)
    f = each(lambda u: jnp.dot(jnp.square(jnp.maximum(u, 0.0)).astype(BF16), wdn_ref[...],
                               preferred_element_type=F32), up)
    x2 = each(lambda v, fv: v + _rms_norm(fv, g3_ref[...]), x1, f)

    gate = each(lambda v: _sigmoid(jnp.dot(_rms_norm(v, g4_ref[...]).astype(BF16), wpg_ref[...],
                                           preferred_element_type=F32)), x2)
    proj = each(lambda rows: jnp.dot(p_ref[0, rows, :].astype(BF16), wpp_ref[...],
                                     preferred_element_type=F32), groups)
    for rows, v, gt, pj in zip(groups, x2, gate, proj):
        out_ref[0, rows, :] = v + gt * pj


def _post_call(x, ot, gate, gconv, p, w_ao, w_o, w_up, w_dn, w_pg, w_pp, g1, g2, g3, g4):
    bsz, seq, _ = x.shape
    tm = TOKEN_TILE
    n_tiles = seq // tm
    tok = lambda width: pl.BlockSpec((1, tm, width), lambda b, i: (b, i, 0))
    flat = pl.BlockSpec((tm, D_MODEL), lambda b, i: (b * n_tiles + i, 0))
    return pl.pallas_call(
        _post_kernel,
        out_shape=jax.ShapeDtypeStruct((bsz, seq, D_MODEL), F32),
        grid=(bsz, seq // tm),
        in_specs=[
            tok(D_MODEL),
            pl.BlockSpec((1, ATTN_WIDTH, tm), lambda b, i: (b, 0, i)),
            flat,
            flat,
            tok(PLE_DIM),
            _resident((ATTN_WIDTH, D_MODEL)),
            _resident((D_MODEL, D_MODEL)),
            _resident((D_MODEL, D_FF)),
            _resident((D_FF, D_MODEL)),
            _resident((D_MODEL, D_MODEL)),
            _resident((PLE_DIM, D_MODEL)),
            _resident((1, D_MODEL)),
            _resident((1, D_MODEL)),
            _resident((1, D_MODEL)),
            _resident((1, D_MODEL)),
        ],
        out_specs=tok(D_MODEL),
        compiler_params=pltpu.CompilerParams(
            dimension_semantics=("arbitrary", "arbitrary"),
            vmem_limit_bytes=VMEM_LIMIT_BYTES),
        name="post_mix_mlp",
    )(x, ot, gate, gconv, p, w_ao, w_o, w_up, w_dn, w_pg, w_pp, g1, g2, g3, g4)


def kernel(x, p, g_pre_mix, w_in, b_gate, w_conv, w_attn_out, w_conv_out, w_o,
           g_post_mix, g_pre_mlp, w_up, w_down, g_post_mlp, g_ple, w_ple_gate, w_ple_proj):
    bsz, seq, _ = x.shape
    depth = w_in.shape[0]
    row = lambda v: v.reshape(1, -1)
    for i in range(depth):
        gate, gconv, ot = _front_call(
            x, row(g_pre_mix[i]), w_in[i].astype(BF16), row(b_gate[i]), w_conv[i],
            w_conv_out[i].astype(BF16))
        ot = ot.reshape(bsz, ATTN_WIDTH, seq)
        x = _post_call(
            x, ot, gate, gconv, p[i],
            w_attn_out[i].astype(BF16), w_o[i].astype(BF16), w_up[i].astype(BF16),
            w_down[i].astype(BF16), w_ple_gate[i].astype(BF16), w_ple_proj[i].astype(BF16),
            row(g_post_mix[i]), row(g_pre_mlp[i]), row(g_post_mlp[i]), row(g_ple[i]))
    return x
```

```python
import functools
import math

import jax
import jax.numpy as jnp
from jax import lax
from jax.experimental import pallas as pl
from jax.experimental.pallas import tpu as pltpu

D_MODEL = 1024
N_HEADS = 8
HEAD_DIM = 64
ATTN_WIDTH = N_HEADS * HEAD_DIM
CONV_WIDTH = 512
CONV_K = 3
D_FF = 4 * D_MODEL
PLE_DIM = 256
EPS = 1e-6

LANES = 128
SUBLANES = 8
VMEM_LIMIT_BYTES = 58 * 1024 * 1024

TOKEN_TILE = 512
ROW_GROUPS = 2
KEY_BLOCK = 256
HEAD_PAD = 2 * HEAD_DIM
ATTN_HEAD_GROUP = 4
CAST_MIN_ROWS = 16
EXP2_UNDERFLOW = 160.0

BF16 = jnp.bfloat16
F32 = jnp.float32


def _rms_norm(xf, g):
    var = jnp.mean(xf * xf, axis=-1, keepdims=True)
    return xf * lax.rsqrt(var + EPS) * g


def _sigmoid(v):
    return 1.0 / (1.0 + jnp.exp(-v))


def _resident(shape):
    zeros = (0,) * len(shape)
    return pl.BlockSpec(shape, lambda *_: zeros, pipeline_mode=pl.Buffered(1))


def _softplus2(z):
    return jnp.maximum(z, 0) + jnp.log2(1 + jnp.exp2(-jnp.abs(z)))


def _front_kernel(x_ref, g_ref, w_ref, bg_ref, wc_ref, wco_ref,
                  f0_ref, f1_ref, f2_ref, f3_ref, f4_ref, f5_ref,
                  gate_ref, gconv_ref, o_ref,
                  b0_ref, b1_ref, b2_ref, b3_ref, b4_ref, b5_ref,
                  wt_ref, ubuf_ref, kall_ref, vall_ref, q_ref,
                  z_ref, d_ref, tot_ref, carry_ref, ahead_ref, acc_ref, *, q_scale, n_tiles):
    tm = x_ref.shape[1]
    bk = KEY_BLOCK
    tg = bk
    groups = range(tm // tg)
    seq = n_tiles * tm
    t = pl.program_id(0)
    last = pl.num_programs(0) - 2
    tp = jnp.minimum(t, last)
    batch, i = tp // n_tiles, tp % n_tiles
    ta = jnp.maximum(t - 1, 0)
    batch_a, i_a = ta // n_tiles, ta % n_tiles
    q_write = jnp.bitwise_and(t, 1)
    first_row = jnp.where(jnp.bitwise_and(batch, 1) == 0, seq, 0)
    first_row_a = jnp.where(jnp.bitwise_and(batch_a, 1) == 0, seq, 0)
    k0, v0, c0 = ATTN_WIDTH, 2 * ATTN_WIDTH, 3 * ATTN_WIDTH
    g0 = c0 + 3 * CONV_WIDTH

    row = lax.broadcasted_iota(jnp.int32, (bk, bk), 0)
    col = lax.broadcasted_iota(jnp.int32, (bk, bk), 1)
    neg_tri = jnp.where(col >= row, -1.0, 0.0).astype(BF16)

    @pl.when(i == 0)
    def _():
        ubuf_ref[0:SUBLANES, :] = jnp.zeros((SUBLANES, CONV_WIDTH), F32)

    @pl.when(t == 0)
    def _():
        q_ref[1] = jnp.zeros(q_ref.shape[1:], BF16)
        wt_ref[0:ATTN_WIDTH, :] = w_ref[:, 0:ATTN_WIDTH].T
        wt_ref[ATTN_WIDTH:2 * ATTN_WIDTH, :] = w_ref[:, v0:v0 + ATTN_WIDTH].T
        kall_ref[seq:seq + bk, :] = jnp.zeros((bk, ATTN_WIDTH), BF16)
        vall_ref[:, seq // bk] = jnp.zeros((N_HEADS, HEAD_DIM, bk), BF16)

    def norm(g):
        return _rms_norm(x_ref[0, pl.ds(g * tg, tg), :], g_ref[...]).astype(BF16)

    def qkv(g, h):
        cols = pl.ds(g * tg, tg)
        start = pl.multiple_of(jnp.where(i == 0, first_row, i * tm) + g * tg, tg)
        tr = lax.dot_general(wt_ref[...], h, (((1,), (1,)), ((), ())),
                             preferred_element_type=F32)
        zeros_half = jnp.zeros((HEAD_DIM, tg), BF16)
        for hd in range(N_HEADS):
            q_h = (tr[hd * HEAD_DIM:(hd + 1) * HEAD_DIM] * q_scale).astype(BF16)
            lo = (hd % 2) * HEAD_DIM
            q_ref[q_write, hd, lo:lo + HEAD_DIM, cols] = q_h
            q_ref[q_write, hd, HEAD_DIM - lo:2 * HEAD_DIM - lo, cols] = zeros_half
            vall_ref[hd, start // bk] = tr[ATTN_WIDTH + hd * HEAD_DIM:
                                           ATTN_WIDTH + (hd + 1) * HEAD_DIM].astype(BF16)
        kall_ref[pl.ds(start, tg), :] = jnp.dot(h, w_ref[:, k0:k0 + ATTN_WIDTH],
                                                preferred_element_type=F32).astype(BF16)

    def conv_inputs(h):
        return jnp.dot(h, w_ref[:, c0:c0 + 3 * CONV_WIDTH], preferred_element_type=F32)

    def conv_branch(g, conv3):
        r0 = SUBLANES + g * tg
        cb = conv3[:, 0:CONV_WIDTH]
        u = conv3[:, CONV_WIDTH:2 * CONV_WIDTH] * conv3[:, 2 * CONV_WIDTH:3 * CONV_WIDTH]
        ubuf_ref[r0:r0 + tg, :] = u
        u1 = ubuf_ref[r0 - 1:r0 - 1 + tg, :]
        u2 = ubuf_ref[r0 - 2:r0 - 2 + tg, :]
        wc = wc_ref[...]
        conv = wc[0:1, :] * u2 + wc[1:2, :] * u1 + wc[2:3, :] * u
        return jnp.dot((cb * conv).astype(BF16), wco_ref[...],
                       preferred_element_type=F32)

    def gate_attn(g, h):
        ga = jnp.dot(h, w_ref[:, g0:g0 + D_MODEL], preferred_element_type=F32)
        gate_ref[pl.ds(g * tg, tg), :] = _sigmoid(ga + bg_ref[:, 0:D_MODEL]).astype(BF16)

    def gate_conv(g, h, y_conv):
        gc = jnp.dot(h, w_ref[:, g0 + D_MODEL:g0 + 2 * D_MODEL], preferred_element_type=F32)
        gconv_ref[pl.ds(g * tg, tg), :] = (_sigmoid(gc + bg_ref[:, D_MODEL:2 * D_MODEL])
                                           * y_conv).astype(BF16)

    calls = [(a, hg) for a in range(tm // bk) for hg in range(N_HEADS // ATTN_HEAD_GROUP)]
    local = range(ATTN_HEAD_GROUP)

    def call_top(a):
        return jnp.where(t == 0, -1, (tm // bk) * i_a + a)

    def block_row(top, m):
        blk = jnp.maximum(top - m, 0)
        return pl.multiple_of(jnp.where(blk < tm // bk, first_row_a, 0) + blk * bk, bk)

    def key_block(top, m, hd):
        pair = (hd // 2) * HEAD_PAD
        return kall_ref[pl.ds(block_row(top, m), bk), pair:pair + HEAD_PAD]

    def value_block(top, m, hd):
        return vall_ref[hd, block_row(top, m) // bk]

    def query_tile(a, hd):
        return q_ref[1 - q_write, hd, :, a * bk:(a + 1) * bk]

    def cumsum_of(z):
        ncsum = jnp.dot(neg_tri, _softplus2(z).astype(BF16), preferred_element_type=F32)
        return ncsum + z, ncsum[0:1, :]

    z_vals, d_vals = {}, {}

    def scores01(c):
        a, hg = calls[c]
        top = call_top(a)
        out = []
        for hl in local:
            hd = hg * ATTN_HEAD_GROUP + hl
            q = query_tile(a, hd)
            z0 = jnp.dot(key_block(top, 0, hd), q, preferred_element_type=F32)
            z0 = jnp.where(row < col, z0, -jnp.inf)
            z1 = jnp.dot(key_block(top, 1, hd), q, preferred_element_type=F32)
            out.append((z0, z1))
        z_vals[c] = out

    def cumsum01(c):
        d_vals[c] = [cumsum_of(z0) + cumsum_of(z1) for z0, z1 in z_vals[c]]

    def weights01(c):
        a, hg = calls[c]
        top = call_top(a)
        for hl in local:
            hd = hg * ATTN_HEAD_GROUP + hl
            d0, t0, d1, t1 = d_vals[c][hl]
            pv0 = jnp.dot(value_block(top, 0, hd), jnp.exp2(d0).astype(BF16),
                          preferred_element_type=F32)
            pv1 = jnp.dot(value_block(top, 1, hd), jnp.exp2(d1).astype(BF16),
                          preferred_element_type=F32)
            acc_ref[c, hl] = (jnp.where(top >= 0, pv0, 0.0)
                              + jnp.where(top >= 1, pv1 * jnp.exp2(t0), 0.0))
            carry_ref[c, hl] = t0 + t1
            ahead_ref[c, hl] = t0 + t1

    def further_blocks(c):
        a, hg = calls[c]
        top = call_top(a)

        def more_after(m):
            return jnp.logical_and(m + 1 <= top, jnp.max(ahead_ref[c]) >= -EXP2_UNDERFLOW)

        def scores(m, par):
            for hl in local:
                hd = hg * ATTN_HEAD_GROUP + hl
                z_ref[par, hl] = jnp.dot(key_block(top, m, hd), query_tile(a, hd),
                                         preferred_element_type=F32)

        def cumsum(par):
            for hl in local:
                d, t = cumsum_of(z_ref[par, hl])
                d_ref[par, hl] = d
                tot_ref[par, hl] = t
                ahead_ref[c, hl] += t

        def weights(m, par):
            exists = m <= top
            for hl in local:
                hd = hg * ATTN_HEAD_GROUP + hl
                pv = jnp.dot(value_block(top, m, hd), jnp.exp2(d_ref[par, hl]).astype(BF16),
                             preferred_element_type=F32)
                ncarry = carry_ref[c, hl]
                acc_ref[c, hl] += jnp.where(exists, pv * jnp.exp2(ncarry), 0.0)
                carry_ref[c, hl] = ncarry + tot_ref[par, hl]

        @pl.when(more_after(1))
        def _():
            scores(2, 0)
            scores(3, 1)
            cumsum(0)

            def step(t, par):
                scores(t + 2, par)
                cumsum(1 - par)
                weights(t, par)
                return more_after(t + 1)

            def body(state):
                t, _ = state
                more = step(t, 0)
                more_after_odd = lax.cond(more, lambda: step(t + 1, 1), lambda: jnp.bool_(False))
                return jnp.where(more, t + 2, t + 1), more_after_odd

            last, _ = lax.while_loop(lambda state: state[1], body, (jnp.int32(2), True))

            @pl.when(jnp.bitwise_and(last, 1) == 0)
            def _():
                weights(last, 0)

            @pl.when(jnp.bitwise_and(last, 1) == 1)
            def _():
                weights(last, 1)

    hs = [norm(g) for g in groups]
    scores01(0)
    qkv(0, hs[0])
    cumsum01(0)
    qkv(1, hs[1])
    weights01(0)
    scores01(1)
    conv3_0 = conv_inputs(hs[0])
    cumsum01(1)
    conv3_1 = conv_inputs(hs[1])
    weights01(1)
    scores01(2)
    y_conv = [conv_branch(0, conv3_0), conv_branch(1, conv3_1)]
    ubuf_ref[0:SUBLANES, :] = ubuf_ref[tm:tm + SUBLANES, :]
    cumsum01(2)
    gate_attn(0, hs[0])
    weights01(2)
    scores01(3)
    gate_attn(1, hs[1])
    cumsum01(3)
    gate_conv(0, hs[0], y_conv[0])
    weights01(3)
    gate_conv(1, hs[1], y_conv[1])

    last_top = call_top(tm // bk - 1)
    @pl.when(jnp.logical_and(2 <= last_top, jnp.max(ahead_ref[...]) >= -EXP2_UNDERFLOW))
    def _():
        for c in range(len(calls)):
            further_blocks(c)

    for c, (a, hg) in enumerate(calls):
        for hl in local:
            o_ref[0, hg * ATTN_HEAD_GROUP + hl, :, a * bk:(a + 1) * bk] = acc_ref[c, hl].astype(BF16)

    for src, dst in ((f0_ref, b0_ref), (f1_ref, b1_ref), (f2_ref, b2_ref),
                     (f3_ref, b3_ref), (f4_ref, b4_ref), (f5_ref, b5_ref)):
        dst[...] = src[...].astype(BF16)


def _front_call(x, g, w, b_gate, w_conv, w_conv_out, later_weights):
    bsz, seq, _ = x.shape
    tm = TOKEN_TILE
    n_tiles = seq // tm
    total = bsz * n_tiles
    n_calls = (tm // KEY_BLOCK) * (N_HEADS // ATTN_HEAD_GROUP)
    q_scale = (HEAD_DIM ** -0.5) * math.log2(math.e)
    out_shape = (
        jax.ShapeDtypeStruct((bsz * seq + tm, D_MODEL), BF16),
        jax.ShapeDtypeStruct((bsz * seq + tm, D_MODEL), BF16),
        jax.ShapeDtypeStruct((bsz, N_HEADS, HEAD_DIM, seq), BF16),
    ) + tuple(jax.ShapeDtypeStruct(lw.shape, BF16) for lw in later_weights)

    def cast_slice(lw):
        rows = max(lw.shape[0] // total, CAST_MIN_ROWS)
        last = lw.shape[0] // rows - 1
        return pl.BlockSpec((rows, D_MODEL), lambda t: (jnp.minimum(t, last), 0))

    def projected(t):
        tp = jnp.minimum(t, total - 1)
        return tp // n_tiles, tp % n_tiles

    def attended(t):
        ta = jnp.maximum(t - 1, 0)
        return ta // n_tiles, ta % n_tiles

    return pl.pallas_call(
        functools.partial(_front_kernel, q_scale=q_scale, n_tiles=n_tiles),
        out_shape=out_shape,
        grid=(total + 1,),
        in_specs=[
            pl.BlockSpec((1, tm, D_MODEL), lambda t: (*projected(t), 0)),
            _resident((1, D_MODEL)),
            _resident(w.shape),
            _resident((1, 2 * D_MODEL)),
            _resident((CONV_K, CONV_WIDTH)),
            _resident((CONV_WIDTH, D_MODEL)),
        ] + [cast_slice(lw) for lw in later_weights],
        out_specs=(
            pl.BlockSpec((tm, D_MODEL), lambda t: (t, 0)),
            pl.BlockSpec((tm, D_MODEL), lambda t: (t, 0)),
            pl.BlockSpec((1, N_HEADS, HEAD_DIM, tm),
                         lambda t: (attended(t)[0], 0, 0, attended(t)[1])),
        ) + tuple(cast_slice(lw) for lw in later_weights),
        scratch_shapes=[
            pltpu.VMEM((2 * ATTN_WIDTH, D_MODEL), BF16),
            pltpu.VMEM((tm + SUBLANES, CONV_WIDTH), F32),
            pltpu.VMEM((seq + tm, ATTN_WIDTH), BF16),
            pltpu.VMEM((N_HEADS, (seq + tm) // KEY_BLOCK, HEAD_DIM, KEY_BLOCK), BF16),
            pltpu.VMEM((2, N_HEADS, HEAD_PAD, tm), BF16),
            pltpu.VMEM((2, ATTN_HEAD_GROUP, KEY_BLOCK, KEY_BLOCK), F32),
            pltpu.VMEM((2, ATTN_HEAD_GROUP, KEY_BLOCK, KEY_BLOCK), F32),
            pltpu.VMEM((2, ATTN_HEAD_GROUP, 1, KEY_BLOCK), F32),
            pltpu.VMEM((n_calls, ATTN_HEAD_GROUP, 1, KEY_BLOCK), F32),
            pltpu.VMEM((n_calls, ATTN_HEAD_GROUP, 1, KEY_BLOCK), F32),
            pltpu.VMEM((n_calls, ATTN_HEAD_GROUP, HEAD_DIM, KEY_BLOCK), F32),
        ],
        compiler_params=pltpu.CompilerParams(
            dimension_semantics=("arbitrary",),
            vmem_limit_bytes=VMEM_LIMIT_BYTES),
        name="front_proj_attn",
    )(x, g, w, b_gate, w_conv, w_conv_out, *later_weights)


def _post_kernel(x_ref, ot_ref, gate_ref, gconv_ref, p_ref,
                 wao_ref, wo_ref, wup_ref, wdn_ref, wpg_ref, wpp_ref,
                 g1_ref, g2_ref, g3_ref, g4_ref, out_ref):
    tm = x_ref.shape[1]
    groups = [pl.ds(r * (tm // ROW_GROUPS), tm // ROW_GROUPS) for r in range(ROW_GROUPS)]

    def each(fn, *lists):
        return [fn(*args) for args in zip(*lists)]

    y_attn = each(lambda rows: lax.dot_general(
        ot_ref[0, :, rows], wao_ref[...], (((0,), (0,)), ((), ())),
        preferred_element_type=F32), groups)
    mixed = each(lambda rows, y: jnp.dot(
        (gate_ref[rows, :] * y + gconv_ref[rows, :]).astype(BF16), wo_ref[...],
        preferred_element_type=F32), groups, y_attn)
    x1 = each(lambda rows, m: x_ref[0, rows, :] + _rms_norm(m, g1_ref[...]), groups, mixed)

    h2 = each(lambda v: _rms_norm(v, g2_ref[...]).astype(BF16), x1)
    up = each(lambda h: jnp.dot(h, wup_ref[...], preferred_element_type=F32), h2)
    f = each(lambda u: jnp.dot(jnp.square(jnp.maximum(u, 0.0)).astype(BF16), wdn_ref[...],
                               preferred_element_type=F32), up)
    x2 = each(lambda v, fv: v + _rms_norm(fv, g3_ref[...]), x1, f)

    gate = each(lambda v: _sigmoid(jnp.dot(_rms_norm(v, g4_ref[...]).astype(BF16), wpg_ref[...],
                                           preferred_element_type=F32)), x2)
    proj = each(lambda rows: jnp.dot(p_ref[0, rows, :].astype(BF16), wpp_ref[...],
                                     preferred_element_type=F32), groups)
    for rows, v, gt, pj in zip(groups, x2, gate, proj):
        out_ref[0, rows, :] = v + gt * pj


def _post_call(x, ot, gate, gconv, p, w_ao, w_o, w_up, w_dn, w_pg, w_pp, g1, g2, g3, g4):
    bsz, seq, _ = x.shape
    tm = TOKEN_TILE
    n_tiles = seq // tm
    tok = lambda width: pl.BlockSpec((1, tm, width), lambda b, i: (b, i, 0))
    flat = pl.BlockSpec((tm, D_MODEL), lambda b, i: (b * n_tiles + i, 0))
    return pl.pallas_call(
        _post_kernel,
        out_shape=jax.ShapeDtypeStruct((bsz, seq, D_MODEL), F32),
        grid=(bsz, seq // tm),
        in_specs=[
            tok(D_MODEL),
            pl.BlockSpec((1, ATTN_WIDTH, tm), lambda b, i: (b, 0, i)),
            flat,
            flat,
            tok(PLE_DIM),
            _resident((ATTN_WIDTH, D_MODEL)),
            _resident((D_MODEL, D_MODEL)),
            _resident((D_MODEL, D_FF)),
            _resident((D_FF, D_MODEL)),
            _resident((D_MODEL, D_MODEL)),
            _resident((PLE_DIM, D_MODEL)),
            _resident((1, D_MODEL)),
            _resident((1, D_MODEL)),
            _resident((1, D_MODEL)),
            _resident((1, D_MODEL)),
        ],
        out_specs=tok(D_MODEL),
        compiler_params=pltpu.CompilerParams(
            dimension_semantics=("arbitrary", "arbitrary"),
            vmem_limit_bytes=VMEM_LIMIT_BYTES),
        name="post_mix_mlp",
    )(x, ot, gate, gconv, p, w_ao, w_o, w_up, w_dn, w_pg, w_pp, g1, g2, g3, g4)


def kernel(x, p, g_pre_mix, w_in, b_gate, w_conv, w_attn_out, w_conv_out, w_o,
           g_post_mix, g_pre_mlp, w_up, w_down, g_post_mlp, g_ple, w_ple_gate, w_ple_proj):
    bsz, seq, _ = x.shape
    depth = w_in.shape[0]
    row = lambda v: v.reshape(1, -1)
    for i in range(depth):
        later = (w_attn_out[i], w_o[i], w_up[i], w_down[i], w_ple_gate[i], w_ple_proj[i])
        gate, gconv, ot, *later_bf16 = _front_call(
            x, row(g_pre_mix[i]), w_in[i].astype(BF16), row(b_gate[i]), w_conv[i],
            w_conv_out[i].astype(BF16), [lw.reshape(-1, D_MODEL) for lw in later])
        ot = ot.reshape(bsz, ATTN_WIDTH, seq)
        x = _post_call(
            x, ot, gate, gconv, p[i],
            *[lb.reshape(lw.shape) for lb, lw in zip(later_bf16, later)],
            row(g_post_mix[i]), row(g_pre_mlp[i]), row(g_post_mlp[i]), row(g_ple[i]))
    return x
```
